```python
import jax, jax.numpy as jnp
from jax import lax
import numpy as np

D_MODEL = 1024
BATCH = 8
SEQ = 8192
DEPTH = 4

CHUNK = 64
MIX_WIDTH = D_MODEL
N_MIXERS = 4
GROUP_WIDTH = MIX_WIDTH // N_MIXERS
POOL_WINDOWS = (2, 4, 8, 16)
POOL_GROUP = GROUP_WIDTH // len(POOL_WINDOWS)
SGU_HEADS = 4
SGU_BLOCK = 128
SGU_HEAD_DIM = GROUP_WIDTH // SGU_HEADS
LRU_BLOCKS = 4
LRU_BLOCK_DIM = GROUP_WIDTH // LRU_BLOCKS
CONV_WIDTH = 4
LRU_C = 8.0
HGRN_HEADS = 4
HGRN_HEAD_DIM = GROUP_WIDTH // HGRN_HEADS
N_IN_SLICES = 9
IN_WIDTH = N_IN_SLICES * GROUP_WIDTH
D_FF = 7 * D_MODEL // 2
N_EXPERTS = 8
TOP_K = 2
N_DENSE = (DEPTH + 1) // 2
N_MOE = DEPTH // 2
EPS = 1e-6
MIN_FORGET = 1e-30

kernel_name = 'hybrid_stream_block'


def rms_norm(x, g):
    xf = x.astype(jnp.float32)
    y = xf * lax.rsqrt(jnp.mean(xf * xf, axis=-1, keepdims=True) + EPS)
    return (y * g.astype(jnp.float32)).astype(x.dtype)


def modulate(h, shift, scale):
    return h * (1.0 + scale[:, None, :]) + shift[:, None, :]


def swiglu(h, w1, w3, w2):
    return (jax.nn.silu(h @ w1) * (h @ w3)) @ w2


def pool_mixer(x, w, scale):
    bsz, slen, _ = x.shape
    xf = x.astype(jnp.float32)
    cs = jnp.pad(jnp.cumsum(xf, axis=1), ((0, 0), (1, 0), (0, 0)))
    pos = jnp.arange(1, slen + 1, dtype=jnp.float32)[None, :, None]
    means = []
    for gi, win in enumerate(POOL_WINDOWS):
        csg = cs[..., gi * POOL_GROUP:(gi + 1) * POOL_GROUP]
        lo = jnp.pad(csg, ((0, 0), (win - 1, 0), (0, 0)))[:, :slen]
        means.append((csg[:, 1:] - lo) / jnp.minimum(pos, float(win)))
    d = (jnp.concatenate(means, axis=-1) - xf).reshape(bsz, slen, len(POOL_WINDOWS), POOL_GROUP)
    y = jnp.einsum('bsgi,gij->bsgj', d, w.astype(jnp.float32)).reshape(bsz, slen, GROUP_WIDTH)
    return (y * scale.astype(jnp.float32)).astype(x.dtype)


def sgu_mixer(u, v, ws, bias):
    bsz, slen, _ = u.shape
    u = jax.nn.gelu(u)
    vf = jax.nn.gelu(v).astype(jnp.float32)
    mu = jnp.mean(vf, axis=-1, keepdims=True)
    var = jnp.mean(jnp.square(vf - mu), axis=-1, keepdims=True)
    vn = ((vf - mu) * lax.rsqrt(var + EPS)).astype(u.dtype)
    vb = vn.reshape(bsz, slen // SGU_BLOCK, SGU_BLOCK, SGU_HEADS, SGU_HEAD_DIM)
    tri = jnp.tril(jnp.ones((SGU_BLOCK, SGU_BLOCK), dtype=ws.dtype))
    z = jnp.einsum('hts,bnshd->bnthd', ws * tri, vb) + bias.T[None, None, :, :, None]
    return u * z.reshape(bsz, slen, GROUP_WIDTH)


def _linear_combine(left, right):
    a1, b1 = left
    a2, b2 = right
    return a1 * a2, a2 * b1 + b2


def lru_mixer(xb, gate, conv_w, conv_b, wa, ba, wx, bx, lam):
    bsz, slen, _ = xb.shape
    xp = jnp.pad(xb, ((0, 0), (CONV_WIDTH - 1, 0), (0, 0)))
    xc = conv_b + sum(conv_w[k] * xp[:, k:k + slen] for k in range(CONV_WIDTH))
    xh = xc.reshape(bsz, slen, LRU_BLOCKS, LRU_BLOCK_DIM)
    r = jax.nn.sigmoid(jnp.einsum('bshi,hij->bshj', xh, wa).reshape(bsz, slen, GROUP_WIDTH) + ba)
    i = jax.nn.sigmoid(jnp.einsum('bshi,hij->bshj', xh, wx).reshape(bsz, slen, GROUP_WIDTH) + bx)
    log_a = LRU_C * r.astype(jnp.float32) * jax.nn.log_sigmoid(lam.astype(jnp.float32))
    a = jnp.exp(log_a)
    b = jnp.sqrt(jnp.maximum(-jnp.expm1(2.0 * log_a), 0.0)) * (i * xc).astype(jnp.float32)
    _, h = lax.associative_scan(_linear_combine, (a, b), axis=1)
    return jax.nn.gelu(gate) * h.astype(xb.dtype)


def hgrn_mixer(q, fz, inp, g, lb, norm_g):
    bsz, slen, _ = q.shape
    n_chunks = slen // CHUNK
    lb = lb.astype(jnp.float32)
    z = fz.astype(jnp.float32)
    qf = jax.nn.silu(q.astype(jnp.float32))
    sig = jax.nn.sigmoid(z)
    f = lb + (1.0 - lb) * sig
    log_f = jnp.log(jnp.maximum(f, MIN_FORGET))
    k = (1.0 - lb) * (1.0 - sig)
    v = inp.astype(jnp.float32)

    def to_chunks(t):
        return t.reshape(bsz, n_chunks, CHUNK, HGRN_HEADS, HGRN_HEAD_DIM).transpose(1, 0, 3, 2, 4)

    causal = jnp.tril(jnp.ones((CHUNK, CHUNK), dtype=bool))[:, :, None]
    causal_f = causal.astype(jnp.float32)

    def step(state, xs):
        qc, kc, vc, gc = xs
        b = jnp.cumsum(gc, axis=2)
        o_inter = jnp.einsum('bhtk,bhkv->bhtv', qc * jnp.exp(b), state)
        diff = b[:, :, :, None, :] - b[:, :, None, :, :]
        decay = jnp.exp(jnp.where(causal, diff, 0.0)) * causal_f
        scores = jnp.einsum('bhtk,bhtsk,bhsk->bhts', qc, decay, kc)
        o = o_inter + jnp.einsum('bhts,bhsv->bhtv', scores, vc)
        b_last = b[:, :, -1:, :]
        state = (jnp.exp(b_last[:, :, 0, :])[..., None] * state
                 + jnp.einsum('bhsk,bhsv->bhkv', kc * jnp.exp(b_last - b), vc))
        return state, o

    s0 = jnp.zeros((bsz, HGRN_HEADS, HGRN_HEAD_DIM, HGRN_HEAD_DIM), jnp.float32)
    _, o = lax.scan(step, s0, (to_chunks(qf), to_chunks(k), to_chunks(v), to_chunks(log_f)))
    o = o.transpose(1, 0, 3, 2, 4).reshape(bsz, slen, HGRN_HEADS, HGRN_HEAD_DIM)
    o = o * lax.rsqrt(jnp.mean(o * o, axis=-1, keepdims=True) + EPS)
    o = o * norm_g.astype(jnp.float32).reshape(HGRN_HEADS, HGRN_HEAD_DIM)
    return (o.reshape(bsz, slen, GROUP_WIDTH) * jax.nn.silu(g.astype(jnp.float32))).astype(q.dtype)


def moe_ffn(h, router_w, router_b, w1, w3, w2):
    logits = h.astype(jnp.float32) @ router_w.astype(jnp.float32) + router_b.astype(jnp.float32)
    top_v, top_i = lax.top_k(logits, TOP_K)
    top_w = jax.nn.softmax(top_v, axis=-1)
    combine = jnp.sum(jax.nn.one_hot(top_i, N_EXPERTS, dtype=jnp.float32) * top_w[..., None], axis=-2)
    combine = combine.astype(h.dtype)
    y = jnp.zeros_like(h)
    for e in range(N_EXPERTS):
        y = y + combine[..., e:e + 1] * swiglu(h, w1[e], w3[e], w2[e])
    return y


def setup_inputs(seed: int = 0) -> dict:
    key = jax.random.key(seed)
    ks = jax.random.split(key, 40)
    counter = [0]

    def nxt():
        k = ks[counter[0]]
        counter[0] += 1
        return k

    def nrm(shape, scale):
        return scale * jax.random.normal(nxt(), shape, jnp.float32)

    def gain(shape):
        return 1.0 + nrm(shape, 0.02)

    G = GROUP_WIDTH
    gate_pattern = jnp.repeat(jnp.array([0.0, 0.0, 1.0, 0.0, 0.0, 1.0], jnp.float32), D_MODEL)
    a0 = jax.random.uniform(nxt(), (DEPTH, G), jnp.float32, 0.9, 0.999)
    p = a0 ** (1.0 / LRU_C)
    lam = jnp.log(p) - jnp.log1p(-p)
    return {
        'x': nrm((BATCH, SEQ, D_MODEL), 1.0),
        'c': nrm((BATCH, D_MODEL), 1.0),
        'w_mod': nrm((DEPTH, D_MODEL, 6 * D_MODEL), 0.2 * D_MODEL ** -0.5),
        'b_mod': gate_pattern + nrm((DEPTH, 6 * D_MODEL), 0.02),
        'norm_mix': gain((DEPTH, D_MODEL)),
        'norm_ffn': gain((DEPTH, D_MODEL)),
        'norm_final': gain((D_MODEL,)),
        'w_in': nrm((DEPTH, D_MODEL, IN_WIDTH), D_MODEL ** -0.5),
        'w_out': nrm((DEPTH, MIX_WIDTH, D_MODEL), MIX_WIDTH ** -0.5),
        'pool_w': nrm((DEPTH, len(POOL_WINDOWS), POOL_GROUP, POOL_GROUP), POOL_GROUP ** -0.5),
        'pool_scale': 1.0 + nrm((DEPTH, G), 0.05),
        'pool_out_norm': gain((DEPTH, G)),
        'sgu_ws': nrm((DEPTH, SGU_HEADS, SGU_BLOCK, SGU_BLOCK), 0.5 * SGU_BLOCK ** -0.5),
        'sgu_b': 1.0 + nrm((DEPTH, SGU_HEADS, SGU_BLOCK), 0.02),
        'sgu_out_norm': gain((DEPTH, G)),
        'conv_w': nrm((DEPTH, CONV_WIDTH, G), CONV_WIDTH ** -0.5),
        'conv_b': nrm((DEPTH, G), 0.02),
        'lru_wa': nrm((DEPTH, LRU_BLOCKS, LRU_BLOCK_DIM, LRU_BLOCK_DIM), LRU_BLOCK_DIM ** -0.5),
        'lru_ba': nrm((DEPTH, G), 0.02),
        'lru_wx': nrm((DEPTH, LRU_BLOCKS, LRU_BLOCK_DIM, LRU_BLOCK_DIM), LRU_BLOCK_DIM ** -0.5),
        'lru_bx': nrm((DEPTH, G), 0.02),
        'lru_lambda': lam,
        'lru_out_norm': gain((DEPTH, G)),
        'hgrn_gamma': nrm((DEPTH, G), 0.5),
        'hgrn_out_norm': gain((DEPTH, G)),
        'ffn_w1': nrm((N_DENSE, D_MODEL, D_FF), D_MODEL ** -0.5),
        'ffn_w3': nrm((N_DENSE, D_MODEL, D_FF), D_MODEL ** -0.5),
        'ffn_w2': nrm((N_DENSE, D_FF, D_MODEL), D_FF ** -0.5),
        'router_w': nrm((N_MOE, D_MODEL, N_EXPERTS), D_MODEL ** -0.5),
        'router_b': nrm((N_MOE, N_EXPERTS), 0.01),
        'moe_w1': nrm((N_MOE, N_EXPERTS, D_MODEL, D_FF), D_MODEL ** -0.5),
        'moe_w3': nrm((N_MOE, N_EXPERTS, D_MODEL, D_FF), D_MODEL ** -0.5),
        'moe_w2': nrm((N_MOE, N_EXPERTS, D_FF, D_MODEL), D_FF ** -0.5),
    }


def reference(x, c, w_mod, b_mod, norm_mix, norm_ffn, norm_final, w_in, w_out,
              pool_w, pool_scale, pool_out_norm, sgu_ws, sgu_b, sgu_out_norm,
              conv_w, conv_b, lru_wa, lru_ba, lru_wx, lru_bx, lru_lambda, lru_out_norm,
              hgrn_gamma, hgrn_out_norm, ffn_w1, ffn_w3, ffn_w2,
              router_w, router_b, moe_w1, moe_w3, moe_w2):
    p_layers = jax.nn.softmax(hgrn_gamma.astype(jnp.float32), axis=0)
    lower_bounds = jnp.cumsum(p_layers, axis=0) - p_layers[0:1]
    c_act = jax.nn.silu(c)
    for l in range(DEPTH):
        mod = c_act @ w_mod[l] + b_mod[l]
        sh1, sc1, g1, sh2, sc2, g2 = jnp.split(mod, 6, axis=-1)
        h = modulate(rms_norm(x, norm_mix[l]), sh1, sc1)
        proj = h @ w_in[l]
        pa, pu, pv, px, pg, dq, dfz, di, dg = jnp.split(proj, N_IN_SLICES, axis=-1)
        ya = rms_norm(pool_mixer(pa, pool_w[l], pool_scale[l]), pool_out_norm[l])
        yb = rms_norm(sgu_mixer(pu, pv, sgu_ws[l], sgu_b[l]), sgu_out_norm[l])
        yc = rms_norm(lru_mixer(px, pg, conv_w[l], conv_b[l], lru_wa[l], lru_ba[l],
                                lru_wx[l], lru_bx[l], lru_lambda[l]), lru_out_norm[l])
        yd = hgrn_mixer(dq, dfz, di, dg, lower_bounds[l], hgrn_out_norm[l])
        y = jnp.concatenate([ya, yb, yc, yd], axis=-1) @ w_out[l]
        x = x + g1[:, None, :] * y
        h = modulate(rms_norm(x, norm_ffn[l]), sh2, sc2)
        if l % 2 == 0:
            f = swiglu(h, ffn_w1[l // 2], ffn_w3[l // 2], ffn_w2[l // 2])
        else:
            f = moe_ffn(h, router_w[l // 2], router_b[l // 2], moe_w1[l // 2], moe_w3[l // 2], moe_w2[l // 2])
        x = x + g2[:, None, :] * f
    return rms_norm(x, norm_final)
```

```python
import functools

import jax
import jax.numpy as jnp
from jax import lax
from jax.experimental import pallas as pl
from jax.experimental.pallas import tpu as pltpu

D_MODEL = 1024
GROUP = 256
N_IN_SLICES = 9
POOL_WINDOWS = (2, 4, 8, 16)
POOL_HIST = 16
CONV_WIDTH = 4
CONV_HIST = 8
SGU_BLOCK = 128
HEAD_DIM = 64
CHUNK = 64
LRU_C = 8.0
D_FF = 3584
N_EXPERTS = 8
EPS = 1e-6
MIN_FORGET = 1e-30
ROUTER_LANES = 128
NEG_BIG = -1e30
SUBLANES = 8

MIX_TILE = 512
FFN_TILE_M = 1024
FFN_TILE_F = 512
VMEM_LIMIT = 56 * 1024 * 1024

BF16 = jnp.bfloat16
F32 = jnp.float32


def _dot(a, b):
    return jnp.dot(a, b, preferred_element_type=F32)


def _dot_nt(a, b):
    return lax.dot_general(a, b, (((1,), (1,)), ((), ())), preferred_element_type=F32)


def _dot_tn(a, b):
    return lax.dot_general(a, b, (((0,), (0,)), ((), ())), preferred_element_type=F32)


def _gelu(x):
    return 0.5 * x * (1.0 + jnp.tanh(0.7978845608028654 * (x + 0.044715 * (x * x * x))))


def _sigmoid(x):
    return 1.0 / (1.0 + jnp.exp(-x))


def _silu(x):
    return x * _sigmoid(x)


def _rms(x, g):
    return x * lax.rsqrt(jnp.mean(x * x, axis=-1, keepdims=True) + EPS) * g


def _split_dot(x, m):
    hi = x.astype(BF16)
    lo = (x - hi.astype(F32)).astype(BF16)
    return _dot(hi, m) + _dot(lo, m)


def _mod_kernel(c_ref, w_ref, b_ref, o_ref):
    o_ref[0] = _dot(_silu(c_ref[...]), w_ref[0]) + b_ref[0]


def _modulation(c, w_mod, b_mod):
    depth, d, n = w_mod.shape
    bsz = c.shape[0]
    tn = 1536
    return pl.pallas_call(
        _mod_kernel,
        grid=(depth, n // tn),
        in_specs=[
            pl.BlockSpec((bsz, d), lambda l, j: (0, 0)),
            pl.BlockSpec((1, d, tn), lambda l, j: (l, 0, j)),
            pl.BlockSpec((1, 1, tn), lambda l, j: (l, 0, j)),
        ],
        out_specs=pl.BlockSpec((1, bsz, tn), lambda l, j: (l, 0, j)),
        out_shape=jax.ShapeDtypeStruct((depth, bsz, n), F32),
        compiler_params=pltpu.CompilerParams(vmem_limit_bytes=VMEM_LIMIT),
        name="adaln_modulation",
    )(c, w_mod, b_mod.reshape(depth, 1, n))


def _mix_kernel(x_ref, sh_ref, sc_ref, gate_ref, nw_ref, win_ref, wout_ref,
                poolw_ref, poolscale_ref, poolnorm_ref,
                sguw_ref, sgub_ref, sgunorm_ref,
                convw_ref, convb_ref, lruw_ref, lrub_ref, lrulog_ref, lrunorm_ref,
                lb_ref, hgrnnorm_ref,
                o_ref,
                pool_hist, conv_hist, lru_carry, hgrn_state, scan_a, scan_b):
    tile = x_ref.shape[1]
    j = pl.program_id(1)

    @pl.when(j == 0)
    def _():
        pool_hist[...] = jnp.zeros_like(pool_hist)
        conv_hist[...] = jnp.zeros_like(conv_hist)
        lru_carry[...] = jnp.zeros_like(lru_carry)
        hgrn_state[...] = jnp.zeros_like(hgrn_state)

    x = x_ref[0]
    h = _rms(x, nw_ref[...]) * (1.0 + sc_ref[0]) + sh_ref[0]
    hb = h.astype(BF16)

    def proj(i):
        return _dot(hb, win_ref[:, i * GROUP:(i + 1) * GROUP])

    lane = lax.broadcasted_iota(jnp.int32, (1, GROUP), 1)
    lane_head = lane // HEAD_DIM
    row = lax.broadcasted_iota(jnp.int32, (tile, 1), 0)

    bd_r = lax.broadcasted_iota(jnp.int32, (GROUP, GROUP), 0) // HEAD_DIM
    bd_c = lax.broadcasted_iota(jnp.int32, (GROUP, GROUP), 1) // HEAD_DIM
    bd_mask = (bd_r == bd_c).astype(F32)
    bd_ones = bd_mask.astype(BF16)

    pa = proj(0)
    ext = jnp.concatenate([pool_hist[...], pa], axis=0)
    s2 = ext + pltpu.roll(ext, 1, 0)
    s4 = s2 + pltpu.roll(s2, 2, 0)
    s8 = s4 + pltpu.roll(s4, 4, 0)
    s16 = s8 + pltpu.roll(s8, 8, 0)
    pool_hist[...] = pa[tile - POOL_HIST:, :]
    wsum = jnp.where(lane_head == 0, s2, jnp.where(lane_head == 1, s4,
                     jnp.where(lane_head == 2, s8, s16)))[POOL_HIST:, :]
    win = jnp.where(lane_head == 0, 2.0, jnp.where(lane_head == 1, 4.0,
                    jnp.where(lane_head == 2, 8.0, 16.0)))
    pos = (row + j * tile + 1).astype(F32)
    d = wsum / jnp.minimum(pos, win) - pa
    ya = _dot(d.astype(BF16), poolw_ref[...]) * poolscale_ref[...]
    ya = _rms(ya, poolnorm_ref[...])
    y = _dot(ya.astype(BF16), wout_ref[0:GROUP, :])

    u = _gelu(proj(1))
    vf = _gelu(proj(2))
    mu = jnp.mean(vf, axis=-1, keepdims=True)
    vc = vf - mu
    var = jnp.mean(vc * vc, axis=-1, keepdims=True)
    vn = vc * lax.rsqrt(var + EPS)
    zs = []
    for b in range(tile // SGU_BLOCK):
        vb = vn[b * SGU_BLOCK:(b + 1) * SGU_BLOCK, :]
        stack = jnp.concatenate(
            [jnp.where(lane_head == hh, vb, 0.0) for hh in range(GROUP // HEAD_DIM)], axis=0)
        zs.append(_dot(sguw_ref[...], stack.astype(BF16)) + sgub_ref[...])
    z = jnp.concatenate(zs, axis=0)
    yb = _rms(u * z, sgunorm_ref[...])
    y = y + _dot(yb.astype(BF16), wout_ref[GROUP:2 * GROUP, :])

    px = proj(3)
    ext = jnp.concatenate([conv_hist[...], px], axis=0)
    xc = (convb_ref[...] + convw_ref[3:4, :] * ext + convw_ref[2:3, :] * pltpu.roll(ext, 1, 0)
          + convw_ref[1:2, :] * pltpu.roll(ext, 2, 0) + convw_ref[0:1, :] * pltpu.roll(ext, 3, 0))
    xc = xc[CONV_HIST:, :]
    conv_hist[...] = px[tile - CONV_HIST:, :]
    gates = _dot(xc.astype(BF16), lruw_ref[...]) + lrub_ref[...]
    r = _sigmoid(gates[:, :GROUP])
    ig = _sigmoid(gates[:, GROUP:])
    log_a = r * lrulog_ref[...]
    a = jnp.exp(log_a)
    bb = jnp.sqrt(jnp.maximum(1.0 - jnp.exp(2.0 * log_a), 0.0)) * (ig * xc)
    sub = row % SUBLANES
    for s in (1, 2, 4):
        keep = sub >= s
        bb = jnp.where(keep, a * pltpu.roll(bb, s, 0) + bb, bb)
        a = jnp.where(keep, a * pltpu.roll(a, s, 0), a)
    scan_a[...] = a
    scan_b[...] = bb

    def lru_step(g, carry):
        r0 = pl.multiple_of(g * SUBLANES, SUBLANES)
        h8 = scan_a[pl.ds(r0, SUBLANES), :] * carry + scan_b[pl.ds(r0, SUBLANES), :]
        scan_b[pl.ds(r0, SUBLANES), :] = h8
        return h8[SUBLANES - 1:SUBLANES, :]

    carry = lax.fori_loop(0, tile // SUBLANES, lru_step, lru_carry[0:1, :])
    lru_carry[...] = jnp.broadcast_to(carry, lru_carry.shape)
    yc = _rms(_gelu(proj(4)) * scan_b[...], lrunorm_ref[...])
    y = y + _dot(yc.astype(BF16), wout_ref[2 * GROUP:3 * GROUP, :])

    lb = lb_ref[...]
    qf = _silu(proj(5))
    sig = _sigmoid(proj(6))
    log_f = jnp.log(jnp.maximum(lb + (1.0 - lb) * sig, MIN_FORGET))
    kk = (1.0 - lb) * (1.0 - sig)
    vv = proj(7)
    csub = row % CHUNK
    bcum = log_f
    for s in (1, 2, 4, 8, 16, 32):
        bcum = jnp.where(csub >= s, bcum + pltpu.roll(bcum, s, 0), bcum)
    t_idx = lax.broadcasted_iota(jnp.int32, (CHUNK, GROUP), 0)
    s_idx = lax.broadcasted_iota(jnp.int32, (CHUNK, GROUP), 1) % HEAD_DIM
    causal = (t_idx >= s_idx).astype(F32)
    n_heads = GROUP // HEAD_DIM
    outs = []
    state = hgrn_state[...]
    for c in range(tile // CHUNK):
        sl = slice(c * CHUNK, (c + 1) * CHUNK)
        bc = bcum[sl]
        bmid = bc[CHUNK // 2 - 1:CHUNK // 2, :]
        blast = bc[CHUNK - 1:CHUNK, :]
        qm = qf[sl] * jnp.exp(bc - bmid)
        km = kk[sl] * jnp.exp(bmid - bc)
        qe = qm * jnp.exp(bmid)
        kl = km * jnp.exp(blast - bmid)
        vcb = vv[sl].astype(BF16)
        k_bd = (jnp.concatenate([km] * n_heads, axis=0) * bd_mask).astype(BF16)
        scores = _dot_nt(qm.astype(BF16), k_bd) * causal
        v_bd = (jnp.concatenate([vv[sl]] * n_heads, axis=0) * bd_mask).astype(BF16)
        o = _dot(scores.astype(BF16), v_bd) + _dot_nt(qe.astype(BF16), state.astype(BF16))
        state = state * jnp.exp(blast) + _dot_tn(vcb, kl.astype(BF16)) * bd_mask
        outs.append(o)
    hgrn_state[...] = state
    o = jnp.concatenate(outs, axis=0)
    ms = _split_dot(o * o, bd_ones) * (1.0 / HEAD_DIM)
    o = o * lax.rsqrt(ms + EPS) * hgrnnorm_ref[...]
    yd = o * _silu(proj(8))
    y = y + _dot(yd.astype(BF16), wout_ref[3 * GROUP:4 * GROUP, :])

    o_ref[0] = x + gate_ref[0] * y


def _row(v):
    return v.reshape(1, -1).astype(F32)


def _block_diag(w):
    n, a, b = w.shape
    eye = jnp.eye(n, dtype=w.dtype)
    return (eye[:, None, :, None] * w[:, :, None, :]).reshape(n * a, n * b)


def _token_mix(x, sh, sc, gate, p):
    bsz, slen, d = x.shape
    tile = min(MIX_TILE, slen)
    full = lambda a: pl.BlockSpec(a.shape, lambda b, j: (0,) * a.ndim)
    per_batch = pl.BlockSpec((1, 1, d), lambda b, j: (b, 0, 0))
    x_spec = pl.BlockSpec((1, tile, d), lambda b, j: (b, j, 0))
    params = [p["norm"], p["w_in"], p["w_out"], p["pool_w"], p["pool_scale"], p["pool_norm"],
              p["sgu_w"], p["sgu_b"], p["sgu_norm"], p["conv_w"], p["conv_b"], p["lru_w"],
              p["lru_b"], p["lru_log"], p["lru_norm"], p["hgrn_lb"], p["hgrn_norm"]]
    return pl.pallas_call(
        _mix_kernel,
        grid=(bsz, slen // tile),
        in_specs=[x_spec, per_batch, per_batch, per_batch] + [full(a) for a in params],
        out_specs=x_spec,
        out_shape=jax.ShapeDtypeStruct(x.shape, x.dtype),
        scratch_shapes=[
            pltpu.VMEM((POOL_HIST, GROUP), F32),
            pltpu.VMEM((CONV_HIST, GROUP), F32),
            pltpu.VMEM((SUBLANES, GROUP), F32),
            pltpu.VMEM((GROUP, GROUP), F32),
            pltpu.VMEM((tile, GROUP), F32),
            pltpu.VMEM((tile, GROUP), F32),
        ],
        compiler_params=pltpu.CompilerParams(
            dimension_semantics=("arbitrary", "arbitrary"), vmem_limit_bytes=VMEM_LIMIT),
        name="token_mix",
    )(x, sh, sc, gate, *params)


def _ffn_kernel(x_ref, sh_ref, sc_ref, gate_ref, nw_ref, w1_ref, w3_ref, w2_ref, fw_ref,
                o_ref, h_scr, acc, *, final_norm):
    f = pl.program_id(1)

    @pl.when(f == 0)
    def _():
        h = _rms(x_ref[...], nw_ref[...]) * (1.0 + sc_ref[0]) + sh_ref[0]
        h_scr[...] = h.astype(BF16)
        acc[...] = jnp.zeros_like(acc)

    hb = h_scr[...]
    act = _silu(_dot(hb, w1_ref[...])) * _dot(hb, w3_ref[...])
    acc[...] += _dot(act.astype(BF16), w2_ref[...])

    @pl.when(f == pl.num_programs(1) - 1)
    def _():
        out = x_ref[...] + gate_ref[0] * acc[...]
        if final_norm:
            out = _rms(out, fw_ref[...])
        o_ref[...] = out


def _moe_kernel(x_ref, sh_ref, sc_ref, gate_ref, nw_ref, rw_ref, rb_ref,
                w1_ref, w3_ref, w2_ref, fw_ref, o_ref, h_scr, comb, acc, *, final_norm):
    e = pl.program_id(1)
    f = pl.program_id(2)

    @pl.when((e == 0) & (f == 0))
    def _():
        h = _rms(x_ref[...], nw_ref[...]) * (1.0 + sc_ref[0]) + sh_ref[0]
        h_scr[...] = h.astype(BF16)
        acc[...] = jnp.zeros_like(acc)
        logits = jnp.dot(h, rw_ref[...], preferred_element_type=F32,
                         precision=lax.Precision.HIGHEST) + rb_ref[...]
        lane = lax.broadcasted_iota(jnp.int32, logits.shape, 1)
        v1 = jnp.max(logits, axis=-1, keepdims=True)
        i1 = jnp.min(jnp.where(logits == v1, lane, ROUTER_LANES), axis=-1, keepdims=True)
        rest = jnp.where(lane == i1, NEG_BIG, logits)
        v2 = jnp.max(rest, axis=-1, keepdims=True)
        i2 = jnp.min(jnp.where(rest == v2, lane, ROUTER_LANES), axis=-1, keepdims=True)
        p2 = jnp.exp(v2 - v1)
        w_top = 1.0 / (1.0 + p2)
        comb[...] = jnp.where(lane == i1, w_top, 0.0) + jnp.where(lane == i2, p2 * w_top, 0.0)

    hb = h_scr[...]
    lane = lax.broadcasted_iota(jnp.int32, comb.shape, 1)
    ce = jnp.sum(jnp.where(lane == e, comb[...], 0.0), axis=-1, keepdims=True)
    act = _silu(_dot(hb, w1_ref[0])) * _dot(hb, w3_ref[0])
    acc[...] += ce * _dot(act.astype(BF16), w2_ref[0])

    @pl.when((e == pl.num_programs(1) - 1) & (f == pl.num_programs(2) - 1))
    def _():
        out = x_ref[...] + gate_ref[0] * acc[...]
        if final_norm:
            out = _rms(out, fw_ref[...])
        o_ref[...] = out


def _channel_mix(x, sh, sc, gate, norm_w, weights, final_w, *, final_norm):
    bsz, slen, d = x.shape
    tm = min(FFN_TILE_M, slen)
    per_seq = slen // tm
    x2 = x.reshape(bsz * slen, d)
    n_tiles = bsz * per_seq
    moe = len(weights) == 5
    if moe:
        rw, rb, w1, w3, w2 = weights
        d_ff = w1.shape[2]
    else:
        w1, w3, w2 = weights
        d_ff = w1.shape[1]
    tf = min(FFN_TILE_F, d_ff)
    scratch = [pltpu.VMEM((tm, d), BF16)]
    if moe:
        grid = (n_tiles, w1.shape[0], d_ff // tf)
        x_spec = pl.BlockSpec((tm, d), lambda i, e, f: (i, 0))
        per_batch = pl.BlockSpec((1, 1, d), lambda i, e, f: (i // per_seq, 0, 0))
        row_spec = pl.BlockSpec((1, d), lambda i, e, f: (0, 0))
        in_specs = [x_spec, per_batch, per_batch, per_batch, row_spec,
                    pl.BlockSpec(rw.shape, lambda i, e, f: (0, 0)),
                    pl.BlockSpec(rb.shape, lambda i, e, f: (0, 0)),
                    pl.BlockSpec((1, d, tf), lambda i, e, f: (e, 0, f)),
                    pl.BlockSpec((1, d, tf), lambda i, e, f: (e, 0, f)),
                    pl.BlockSpec((1, tf, d), lambda i, e, f: (e, f, 0)),
                    row_spec]
        args = (x2, sh, sc, gate, norm_w, rw, rb, w1, w3, w2, final_w)
        body = functools.partial(_moe_kernel, final_norm=final_norm)
        scratch.append(pltpu.VMEM((tm, ROUTER_LANES), F32))
        sem = ("arbitrary", "arbitrary", "arbitrary")
        name = "moe_mix"
    else:
        grid = (n_tiles, d_ff // tf)
        x_spec = pl.BlockSpec((tm, d), lambda i, f: (i, 0))
        per_batch = pl.BlockSpec((1, 1, d), lambda i, f: (i // per_seq, 0, 0))
        row_spec = pl.BlockSpec((1, d), lambda i, f: (0, 0))
        in_specs = [x_spec, per_batch, per_batch, per_batch, row_spec,
                    pl.BlockSpec((d, tf), lambda i, f: (0, f)),
                    pl.BlockSpec((d, tf), lambda i, f: (0, f)),
                    pl.BlockSpec((tf, d), lambda i, f: (f, 0)),
                    row_spec]
        args = (x2, sh, sc, gate, norm_w, w1, w3, w2, final_w)
        body = functools.partial(_ffn_kernel, final_norm=final_norm)
        sem = ("arbitrary", "arbitrary")
        name = "ffn_mix"
    scratch.append(pltpu.VMEM((tm, d), F32))
    out = pl.pallas_call(
        body,
        grid=grid,
        in_specs=in_specs,
        out_specs=x_spec,
        out_shape=jax.ShapeDtypeStruct(x2.shape, x2.dtype),
        scratch_shapes=scratch,
        compiler_params=pltpu.CompilerParams(dimension_semantics=sem, vmem_limit_bytes=VMEM_LIMIT),
        name=name,
    )(*args)
    return out.reshape(bsz, slen, d)


def _mix_params(l, norm_mix, w_in, w_out, pool_w, pool_scale, pool_out_norm, sgu_ws, sgu_b,
                sgu_out_norm, conv_w, conv_b, lru_wa, lru_ba, lru_wx, lru_bx, lru_lambda,
                lru_out_norm, lower_bounds, hgrn_out_norm):
    tri = jnp.tril(jnp.ones((SGU_BLOCK, SGU_BLOCK), sgu_ws.dtype))
    sgu_w = jnp.concatenate([sgu_ws[l, hh] * tri for hh in range(sgu_ws.shape[1])], axis=1)
    return {
        "norm": _row(norm_mix[l]),
        "w_in": w_in[l].astype(BF16),
        "w_out": w_out[l].astype(BF16),
        "pool_w": _block_diag(pool_w[l]).astype(BF16),
        "pool_scale": _row(pool_scale[l]),
        "pool_norm": _row(pool_out_norm[l]),
        "sgu_w": sgu_w.astype(BF16),
        "sgu_b": jnp.repeat(sgu_b[l].T, HEAD_DIM, axis=1).astype(F32),
        "sgu_norm": _row(sgu_out_norm[l]),
        "conv_w": conv_w[l].astype(F32),
        "conv_b": _row(conv_b[l]),
        "lru_w": jnp.concatenate([_block_diag(lru_wa[l]), _block_diag(lru_wx[l])], axis=1).astype(BF16),
        "lru_b": _row(jnp.concatenate([lru_ba[l], lru_bx[l]])),
        "lru_log": _row(LRU_C * jax.nn.log_sigmoid(lru_lambda[l].astype(F32))),
        "lru_norm": _row(lru_out_norm[l]),
        "hgrn_lb": _row(lower_bounds[l]),
        "hgrn_norm": _row(hgrn_out_norm[l]),
    }


def kernel(x, c, w_mod, b_mod, norm_mix, norm_ffn, norm_final, w_in, w_out, pool_w, pool_scale, pool_out_norm, sgu_ws, sgu_b, sgu_out_norm, conv_w, conv_b, lru_wa, lru_ba, lru_wx, lru_bx, lru_lambda, lru_out_norm, hgrn_gamma, hgrn_out_norm, ffn_w1, ffn_w3, ffn_w2, router_w, router_b, moe_w1, moe_w3, moe_w2):
    depth = w_mod.shape[0]
    bsz, slen, d = x.shape
    p_layers = jax.nn.softmax(hgrn_gamma.astype(F32), axis=0)
    lower_bounds = jnp.cumsum(p_layers, axis=0) - p_layers[0:1]
    mod = _modulation(c, w_mod, b_mod)
    final_w = _row(norm_final)
    for l in range(depth):
        m = mod[l].reshape(bsz, 6, 1, d)
        sh1, sc1, g1, sh2, sc2, g2 = (m[:, i] for i in range(6))
        p = _mix_params(l, norm_mix, w_in, w_out, pool_w, pool_scale, pool_out_norm, sgu_ws,
                        sgu_b, sgu_out_norm, conv_w, conv_b, lru_wa, lru_ba, lru_wx, lru_bx,
                        lru_lambda, lru_out_norm, lower_bounds, hgrn_out_norm)
        x = _token_mix(x, sh1, sc1, g1, p)
        i = l // 2
        if l % 2 == 0:
            weights = (ffn_w1[i].astype(BF16), ffn_w3[i].astype(BF16), ffn_w2[i].astype(BF16))
        else:
            n_e = router_w.shape[2]
            rw = jnp.pad(router_w[i].astype(F32), ((0, 0), (0, ROUTER_LANES - n_e)))
            rb = jnp.pad(router_b[i].astype(F32), (0, ROUTER_LANES - n_e),
                         constant_values=NEG_BIG).reshape(1, ROUTER_LANES)
            weights = (rw, rb, moe_w1[i].astype(BF16), moe_w3[i].astype(BF16), moe_w2[i].astype(BF16))
        x = _channel_mix(x, sh2, sc2, g2, _row(norm_ffn[l]), weights, final_w,
                         final_norm=(l == depth - 1))
    return x
```

```python
import functools

import jax
import jax.numpy as jnp
from jax import lax
from jax.experimental import pallas as pl
from jax.experimental.pallas import tpu as pltpu

D_MODEL = 1024
GROUP = 256
N_IN_SLICES = 9
POOL_WINDOWS = (2, 4, 8, 16)
POOL_HIST = 16
CONV_WIDTH = 4
CONV_HIST = 8
SGU_BLOCK = 128
HEAD_DIM = 64
CHUNK = 64
LRU_C = 8.0
D_FF = 3584
N_EXPERTS = 8
EPS = 1e-6
MIN_FORGET = 1e-30
ROUTER_LANES = 128
NEG_BIG = -1e30
SUBLANES = 8

MIX_TILE = 512
FFN_TILE_M = 1024
FFN_TILE_F = 512
MOE_TILE = 1024
GATHER_CHUNK = 1024
COMBINE_TILE = 512
VMEM_LIMIT = 56 * 1024 * 1024

BF16 = jnp.bfloat16
F32 = jnp.float32


def _dot(a, b):
    return jnp.dot(a, b, preferred_element_type=F32)


def _dot_nt(a, b):
    return lax.dot_general(a, b, (((1,), (1,)), ((), ())), preferred_element_type=F32)


def _dot_tn(a, b):
    return lax.dot_general(a, b, (((0,), (0,)), ((), ())), preferred_element_type=F32)


def _gelu(x):
    return 0.5 * x * (1.0 + jnp.tanh(0.7978845608028654 * (x + 0.044715 * (x * x * x))))


def _sigmoid(x):
    return 1.0 / (1.0 + jnp.exp(-x))


def _silu(x):
    return x * _sigmoid(x)


def _rms(x, g):
    return x * lax.rsqrt(jnp.mean(x * x, axis=-1, keepdims=True) + EPS) * g


def _split_dot(x, m):
    hi = x.astype(BF16)
    lo = (x - hi.astype(F32)).astype(BF16)
    return _dot(hi, m) + _dot(lo, m)


def _mod_kernel(c_ref, w_ref, b_ref, o_ref):
    o_ref[0] = _dot(_silu(c_ref[...]), w_ref[0]) + b_ref[0]


def _modulation(c, w_mod, b_mod):
    depth, d, n = w_mod.shape
    bsz = c.shape[0]
    tn = 1536
    return pl.pallas_call(
        _mod_kernel,
        grid=(depth, n // tn),
        in_specs=[
            pl.BlockSpec((bsz, d), lambda l, j: (0, 0)),
            pl.BlockSpec((1, d, tn), lambda l, j: (l, 0, j)),
            pl.BlockSpec((1, 1, tn), lambda l, j: (l, 0, j)),
        ],
        out_specs=pl.BlockSpec((1, bsz, tn), lambda l, j: (l, 0, j)),
        out_shape=jax.ShapeDtypeStruct((depth, bsz, n), F32),
        compiler_params=pltpu.CompilerParams(vmem_limit_bytes=VMEM_LIMIT),
        name="adaln_modulation",
    )(c, w_mod, b_mod.reshape(depth, 1, n))


def _mix_kernel(x_ref, sh_ref, sc_ref, gate_ref, nw_ref, win_ref, wout_ref,
                poolw_ref, poolscale_ref, poolnorm_ref,
                sguw_ref, sgub_ref, sgunorm_ref,
                convw_ref, convb_ref, lruw_ref, lrub_ref, lrulog_ref, lrunorm_ref,
                lb_ref, hgrnnorm_ref,
                o_ref,
                pool_hist, conv_hist, lru_carry, hgrn_state, scan_a, scan_b):
    tile = x_ref.shape[1]
    j = pl.program_id(1)

    @pl.when(j == 0)
    def _():
        pool_hist[...] = jnp.zeros_like(pool_hist)
        conv_hist[...] = jnp.zeros_like(conv_hist)
        lru_carry[...] = jnp.zeros_like(lru_carry)
        hgrn_state[...] = jnp.zeros_like(hgrn_state)

    x = x_ref[0]
    h = _rms(x, nw_ref[...]) * (1.0 + sc_ref[0]) + sh_ref[0]
    hb = h.astype(BF16)

    def proj(i):
        return _dot(hb, win_ref[:, i * GROUP:(i + 1) * GROUP])

    lane = lax.broadcasted_iota(jnp.int32, (1, GROUP), 1)
    lane_head = lane // HEAD_DIM
    row = lax.broadcasted_iota(jnp.int32, (tile, 1), 0)

    bd_r = lax.broadcasted_iota(jnp.int32, (GROUP, GROUP), 0) // HEAD_DIM
    bd_c = lax.broadcasted_iota(jnp.int32, (GROUP, GROUP), 1) // HEAD_DIM
    bd_mask = (bd_r == bd_c).astype(F32)
    bd_ones = bd_mask.astype(BF16)

    pa = proj(0)
    ext = jnp.concatenate([pool_hist[...], pa], axis=0)
    s2 = ext + pltpu.roll(ext, 1, 0)
    s4 = s2 + pltpu.roll(s2, 2, 0)
    s8 = s4 + pltpu.roll(s4, 4, 0)
    s16 = s8 + pltpu.roll(s8, 8, 0)
    pool_hist[...] = pa[tile - POOL_HIST:, :]
    wsum = jnp.where(lane_head == 0, s2, jnp.where(lane_head == 1, s4,
                     jnp.where(lane_head == 2, s8, s16)))[POOL_HIST:, :]
    win = jnp.where(lane_head == 0, 2.0, jnp.where(lane_head == 1, 4.0,
                    jnp.where(lane_head == 2, 8.0, 16.0)))
    pos = (row + j * tile + 1).astype(F32)
    d = wsum / jnp.minimum(pos, win) - pa
    ya = _dot(d.astype(BF16), poolw_ref[...]) * poolscale_ref[...]
    ya = _rms(ya, poolnorm_ref[...])
    y = _dot(ya.astype(BF16), wout_ref[0:GROUP, :])

    u = _gelu(proj(1))
    vf = _gelu(proj(2))
    mu = jnp.mean(vf, axis=-1, keepdims=True)
    vc = vf - mu
    var = jnp.mean(vc * vc, axis=-1, keepdims=True)
    vn = vc * lax.rsqrt(var + EPS)
    zs = []
    for b in range(tile // SGU_BLOCK):
        vb = vn[b * SGU_BLOCK:(b + 1) * SGU_BLOCK, :]
        stack = jnp.concatenate(
            [jnp.where(lane_head == hh, vb, 0.0) for hh in range(GROUP // HEAD_DIM)], axis=0)
        zs.append(_dot(sguw_ref[...], stack.astype(BF16)) + sgub_ref[...])
    z = jnp.concatenate(zs, axis=0)
    yb = _rms(u * z, sgunorm_ref[...])
    y = y + _dot(yb.astype(BF16), wout_ref[GROUP:2 * GROUP, :])

    px = proj(3)
    ext = jnp.concatenate([conv_hist[...], px], axis=0)
    xc = (convb_ref[...] + convw_ref[3:4, :] * ext + convw_ref[2:3, :] * pltpu.roll(ext, 1, 0)
          + convw_ref[1:2, :] * pltpu.roll(ext, 2, 0) + convw_ref[0:1, :] * pltpu.roll(ext, 3, 0))
    xc = xc[CONV_HIST:, :]
    conv_hist[...] = px[tile - CONV_HIST:, :]
    gates = _dot(xc.astype(BF16), lruw_ref[...]) + lrub_ref[...]
    r = _sigmoid(gates[:, :GROUP])
    ig = _sigmoid(gates[:, GROUP:])
    log_a = r * lrulog_ref[...]
    a = jnp.exp(log_a)
    bb = jnp.sqrt(jnp.maximum(1.0 - jnp.exp(2.0 * log_a), 0.0)) * (ig * xc)
    sub = row % SUBLANES
    for s in (1, 2, 4):
        keep = sub >= s
        bb = jnp.where(keep, a * pltpu.roll(bb, s, 0) + bb, bb)
        a = jnp.where(keep, a * pltpu.roll(a, s, 0), a)
    scan_a[...] = a
    scan_b[...] = bb

    def lru_step(g, carry):
        r0 = pl.multiple_of(g * SUBLANES, SUBLANES)
        h8 = scan_a[pl.ds(r0, SUBLANES), :] * carry + scan_b[pl.ds(r0, SUBLANES), :]
        scan_b[pl.ds(r0, SUBLANES), :] = h8
        return h8[SUBLANES - 1:SUBLANES, :]

    carry = lax.fori_loop(0, tile // SUBLANES, lru_step, lru_carry[0:1, :])
    lru_carry[...] = jnp.broadcast_to(carry, lru_carry.shape)
    yc = _rms(_gelu(proj(4)) * scan_b[...], lrunorm_ref[...])
    y = y + _dot(yc.astype(BF16), wout_ref[2 * GROUP:3 * GROUP, :])

    lb = lb_ref[...]
    qf = _silu(proj(5))
    sig = _sigmoid(proj(6))
    log_f = jnp.log(jnp.maximum(lb + (1.0 - lb) * sig, MIN_FORGET))
    kk = (1.0 - lb) * (1.0 - sig)
    vv = proj(7)
    csub = row % CHUNK
    bcum = log_f
    for s in (1, 2, 4, 8, 16, 32):
        bcum = jnp.where(csub >= s, bcum + pltpu.roll(bcum, s, 0), bcum)
    t_idx = lax.broadcasted_iota(jnp.int32, (CHUNK, GROUP), 0)
    s_idx = lax.broadcasted_iota(jnp.int32, (CHUNK, GROUP), 1) % HEAD_DIM
    causal = (t_idx >= s_idx).astype(F32)
    n_heads = GROUP // HEAD_DIM
    outs = []
    state = hgrn_state[...]
    for c in range(tile // CHUNK):
        sl = slice(c * CHUNK, (c + 1) * CHUNK)
        bc = bcum[sl]
        bmid = bc[CHUNK // 2 - 1:CHUNK // 2, :]
        blast = bc[CHUNK - 1:CHUNK, :]
        qm = qf[sl] * jnp.exp(bc - bmid)
        km = kk[sl] * jnp.exp(bmid - bc)
        qe = qm * jnp.exp(bmid)
        kl = km * jnp.exp(blast - bmid)
        vcb = vv[sl].astype(BF16)
        k_bd = (jnp.concatenate([km] * n_heads, axis=0) * bd_mask).astype(BF16)
        scores = _dot_nt(qm.astype(BF16), k_bd) * causal
        v_bd = (jnp.concatenate([vv[sl]] * n_heads, axis=0) * bd_mask).astype(BF16)
        o = _dot(scores.astype(BF16), v_bd) + _dot_nt(qe.astype(BF16), state.astype(BF16))
        state = state * jnp.exp(blast) + _dot_tn(vcb, kl.astype(BF16)) * bd_mask
        outs.append(o)
    hgrn_state[...] = state
    o = jnp.concatenate(outs, axis=0)
    ms = _split_dot(o * o, bd_ones) * (1.0 / HEAD_DIM)
    o = o * lax.rsqrt(ms + EPS) * hgrnnorm_ref[...]
    yd = o * _silu(proj(8))
    y = y + _dot(yd.astype(BF16), wout_ref[3 * GROUP:4 * GROUP, :])

    o_ref[0] = x + gate_ref[0] * y


def _row(v):
    return v.reshape(1, -1).astype(F32)


def _block_diag(w):
    n, a, b = w.shape
    eye = jnp.eye(n, dtype=w.dtype)
    return (eye[:, None, :, None] * w[:, :, None, :]).reshape(n * a, n * b)


def _token_mix(x, sh, sc, gate, p):
    bsz, slen, d = x.shape
    tile = min(MIX_TILE, slen)
    full = lambda a: pl.BlockSpec(a.shape, lambda b, j: (0,) * a.ndim)
    per_batch = pl.BlockSpec((1, 1, d), lambda b, j: (b, 0, 0))
    x_spec = pl.BlockSpec((1, tile, d), lambda b, j: (b, j, 0))
    params = [p["norm"], p["w_in"], p["w_out"], p["pool_w"], p["pool_scale"], p["pool_norm"],
              p["sgu_w"], p["sgu_b"], p["sgu_norm"], p["conv_w"], p["conv_b"], p["lru_w"],
              p["lru_b"], p["lru_log"], p["lru_norm"], p["hgrn_lb"], p["hgrn_norm"]]
    return pl.pallas_call(
        _mix_kernel,
        grid=(bsz, slen // tile),
        in_specs=[x_spec, per_batch, per_batch, per_batch] + [full(a) for a in params],
        out_specs=x_spec,
        out_shape=jax.ShapeDtypeStruct(x.shape, x.dtype),
        scratch_shapes=[
            pltpu.VMEM((POOL_HIST, GROUP), F32),
            pltpu.VMEM((CONV_HIST, GROUP), F32),
            pltpu.VMEM((SUBLANES, GROUP), F32),
            pltpu.VMEM((GROUP, GROUP), F32),
            pltpu.VMEM((tile, GROUP), F32),
            pltpu.VMEM((tile, GROUP), F32),
        ],
        compiler_params=pltpu.CompilerParams(
            dimension_semantics=("arbitrary", "arbitrary"), vmem_limit_bytes=VMEM_LIMIT),
        name="token_mix",
    )(x, sh, sc, gate, *params)


def _ffn_kernel(x_ref, sh_ref, sc_ref, gate_ref, nw_ref, w1_ref, w3_ref, w2_ref, fw_ref,
                o_ref, h_scr, acc, *, final_norm):
    f = pl.program_id(1)

    @pl.when(f == 0)
    def _():
        h = _rms(x_ref[...], nw_ref[...]) * (1.0 + sc_ref[0]) + sh_ref[0]
        h_scr[...] = h.astype(BF16)
        acc[...] = jnp.zeros_like(acc)

    hb = h_scr[...]
    act = _silu(_dot(hb, w1_ref[...])) * _dot(hb, w3_ref[...])
    acc[...] += _dot(act.astype(BF16), w2_ref[...])

    @pl.when(f == pl.num_programs(1) - 1)
    def _():
        out = x_ref[...] + gate_ref[0] * acc[...]
        if final_norm:
            out = _rms(out, fw_ref[...])
        o_ref[...] = out


def _route_kernel(x_ref, sh_ref, sc_ref, nw_ref, rw_ref, rb_ref, h_ref, route_ref):
    h = _rms(x_ref[...], nw_ref[...]) * (1.0 + sc_ref[0]) + sh_ref[0]
    h_ref[...] = h.reshape(h_ref.shape)
    logits = jnp.dot(h, rw_ref[...], preferred_element_type=F32,
                     precision=lax.Precision.HIGHEST) + rb_ref[...]
    lane = lax.broadcasted_iota(jnp.int32, logits.shape, 1)
    v1 = jnp.max(logits, axis=-1, keepdims=True)
    i1 = jnp.min(jnp.where(logits == v1, lane, ROUTER_LANES), axis=-1, keepdims=True)
    rest = jnp.where(lane == i1, NEG_BIG, logits)
    v2 = jnp.max(rest, axis=-1, keepdims=True)
    i2 = jnp.min(jnp.where(rest == v2, lane, ROUTER_LANES), axis=-1, keepdims=True)
    p2 = jnp.exp(v2 - v1)
    w_top = 1.0 / (1.0 + p2)
    route_ref[...] = jnp.where(lane == 0, i1.astype(F32), jnp.where(lane == 1, i2.astype(F32),
                               jnp.where(lane == 2, w_top, jnp.where(lane == 3, p2 * w_top, 0.0))))


def _gather_kernel(pad_ref, nu_ref, pos_ref, h_ref, xs_ref, zeros, sem, zsem, *, n_tokens):
    i = pl.program_id(0)
    chunk = pos_ref.shape[2]
    tile = zeros.shape[0]
    n_tiles = xs_ref.shape[0] // tile

    @pl.when(i == 0)
    def _():
        zeros[...] = jnp.zeros_like(zeros)
        fills = [pltpu.make_async_copy(zeros, xs_ref.at[pl.ds(pad_ref[e], tile)], zsem)
                 for e in range(pad_ref.shape[0])]
        for cp in fills:
            cp.start()
        for cp in fills:
            cp.wait()

        def tail(t, carry):
            cp = pltpu.make_async_copy(
                zeros, xs_ref.at[pl.ds(pl.multiple_of(t * tile, tile), tile)], zsem)
            cp.start()
            cp.wait()
            return carry

        lax.fori_loop(nu_ref[0], n_tiles, tail, 0)

    first = i * chunk
    base = first - (first // n_tokens) * n_tokens

    def issue(r, carry):
        pltpu.make_async_copy(h_ref.at[base + r], xs_ref.at[pos_ref[0, 0, r]], sem).start()
        return carry

    lax.fori_loop(0, chunk, issue, 0, unroll=8)
    pltpu.make_async_copy(h_ref.at[pl.ds(0, chunk)], xs_ref.at[pl.ds(0, chunk)], sem).wait()


def _moe_ffn_kernel(te_ref, nu_ref, xs_ref, w1_ref, w3_ref, w2_ref, ys_ref, h_scr, acc):
    i = pl.program_id(0)
    f = pl.program_id(1)

    @pl.when(i < nu_ref[0])
    def _():
        @pl.when(f == 0)
        def _():
            h_scr[...] = xs_ref[:, 0, :].astype(BF16)
            acc[...] = jnp.zeros_like(acc)

        hb = h_scr[...]
        act = _silu(_dot(hb, w1_ref[0])) * _dot(hb, w3_ref[0])
        acc[...] += _dot(act.astype(BF16), w2_ref[0])

        @pl.when(f == pl.num_programs(1) - 1)
        def _():
            ys_ref[...] = acc[...].reshape(ys_ref.shape)

    @pl.when((i >= nu_ref[0]) & (f == 0))
    def _():
        ys_ref[...] = jnp.zeros_like(ys_ref)


def _combine_kernel(p1_ref, p2_ref, x_ref, route_ref, gate_ref, fw_ref, ys_ref, o_ref,
                    ybuf, sem, *, final_norm):
    tm = x_ref.shape[0]

    def issue(r, carry):
        pltpu.make_async_copy(ys_ref.at[p1_ref[0, 0, r]], ybuf.at[0, r], sem).start()
        pltpu.make_async_copy(ys_ref.at[p2_ref[0, 0, r]], ybuf.at[1, r], sem).start()
        return carry

    lax.fori_loop(0, tm, issue, 0, unroll=8)
    for s in range(2):
        pltpu.make_async_copy(ys_ref.at[pl.ds(0, tm)], ybuf.at[s], sem).wait()
    route = route_ref[...]
    y = route[:, 2:3] * ybuf[0, :, 0, :] + route[:, 3:4] * ybuf[1, :, 0, :]
    out = x_ref[...] + gate_ref[0] * y
    if final_norm:
        out = _rms(out, fw_ref[...])
    o_ref[...] = out


def _ffn_mix(x, sh, sc, gate, norm_w, w1, w3, w2, final_w, *, final_norm):
    bsz, slen, d = x.shape
    tm = min(FFN_TILE_M, slen)
    per_seq = slen // tm
    d_ff = w1.shape[1]
    tf = min(FFN_TILE_F, d_ff)
    x2 = x.reshape(bsz * slen, d)
    x_spec = pl.BlockSpec((tm, d), lambda i, f: (i, 0))
    per_batch = pl.BlockSpec((1, 1, d), lambda i, f: (i // per_seq, 0, 0))
    row_spec = pl.BlockSpec((1, d), lambda i, f: (0, 0))
    out = pl.pallas_call(
        functools.partial(_ffn_kernel, final_norm=final_norm),
        grid=(bsz * per_seq, d_ff // tf),
        in_specs=[x_spec, per_batch, per_batch, per_batch, row_spec,
                  pl.BlockSpec((d, tf), lambda i, f: (0, f)),
                  pl.BlockSpec((d, tf), lambda i, f: (0, f)),
                  pl.BlockSpec((tf, d), lambda i, f: (f, 0)),
                  row_spec],
        out_specs=x_spec,
        out_shape=jax.ShapeDtypeStruct(x2.shape, x2.dtype),
        scratch_shapes=[pltpu.VMEM((tm, d), BF16), pltpu.VMEM((tm, d), F32)],
        compiler_params=pltpu.CompilerParams(
            dimension_semantics=("arbitrary", "arbitrary"), vmem_limit_bytes=VMEM_LIMIT),
        name="ffn_mix",
    )(x2, sh, sc, gate, norm_w, w1, w3, w2, final_w)
    return out.reshape(bsz, slen, d)


def _moe_mix(x, sh, sc, gate, norm_w, rw, rb, w1, w3, w2, final_w, *, final_norm):
    bsz, slen, d = x.shape
    n = bsz * slen
    n_e, _, d_ff = w1.shape
    x2 = x.reshape(n, d)

    tr = min(FFN_TILE_M, slen)
    per_seq = slen // tr
    per_batch = pl.BlockSpec((1, 1, d), lambda i: (i // per_seq, 0, 0))
    h3, route = pl.pallas_call(
        _route_kernel,
        grid=(n // tr,),
        in_specs=[pl.BlockSpec((tr, d), lambda i: (i, 0)), per_batch, per_batch,
                  pl.BlockSpec((1, d), lambda i: (0, 0)),
                  pl.BlockSpec(rw.shape, lambda i: (0, 0)),
                  pl.BlockSpec(rb.shape, lambda i: (0, 0))],
        out_specs=[pl.BlockSpec((tr, 1, d), lambda i: (i, 0, 0)),
                   pl.BlockSpec((tr, ROUTER_LANES), lambda i: (i, 0))],
        out_shape=[jax.ShapeDtypeStruct((n, 1, d), F32),
                   jax.ShapeDtypeStruct((n, ROUTER_LANES), F32)],
        compiler_params=pltpu.CompilerParams(vmem_limit_bytes=VMEM_LIMIT),
        name="moe_route",
    )(x2, sh, sc, norm_w, rw, rb)

    tm = min(MOE_TILE, n)
    n_tiles = 2 * n // tm + n_e
    rows = (n_tiles + 1) * tm
    e_flat = jnp.concatenate([route[:, 0], route[:, 1]]).astype(jnp.int32)
    onehot = (e_flat[:, None] == jnp.arange(n_e, dtype=jnp.int32)[None, :]).astype(jnp.int32)
    csum = jnp.cumsum(onehot, axis=0)
    counts = csum[-1]
    padded = ((counts + tm - 1) // tm) * tm
    gstart = jnp.cumsum(padded) - padded
    pos = jnp.sum(onehot * (gstart[None, :] + csum - 1), axis=1).astype(jnp.int32)
    pad_start = (gstart + counts).astype(jnp.int32)
    tile_end = jnp.cumsum(padded // tm)
    n_used = tile_end[-1:].astype(jnp.int32)
    tile_ids = jnp.minimum(jnp.arange(n_tiles, dtype=jnp.int32), n_used[0] - 1)
    tile_expert = jnp.sum((tile_ids[:, None] >= tile_end[None, :]).astype(jnp.int32), axis=1)

    chunk = min(GATHER_CHUNK, n)
    xs = pl.pallas_call(
        functools.partial(_gather_kernel, n_tokens=n),
        grid_spec=pltpu.PrefetchScalarGridSpec(
            num_scalar_prefetch=2,
            grid=(2 * n // chunk,),
            in_specs=[pl.BlockSpec((1, 1, chunk), lambda i, pad, nu: (i, 0, 0),
                                   memory_space=pltpu.SMEM),
                      pl.BlockSpec(memory_space=pl.ANY)],
            out_specs=pl.BlockSpec(memory_space=pl.ANY),
            scratch_shapes=[pltpu.VMEM((tm, 1, d), F32), pltpu.SemaphoreType.DMA(()),
                            pltpu.SemaphoreType.DMA(())]),
        out_shape=jax.ShapeDtypeStruct((rows, 1, d), F32),
        compiler_params=pltpu.CompilerParams(
            dimension_semantics=("arbitrary",), vmem_limit_bytes=VMEM_LIMIT),
        name="moe_gather",
    )(pad_start, n_used, pos.reshape(2 * n // chunk, 1, chunk), h3)

    tf = min(FFN_TILE_F, d_ff)
    n_f = d_ff // tf

    def row_map(i, f, te, nu):
        return (jnp.minimum(i, nu[0] - 1), 0, 0)

    def f_idx(i, f, nu):
        return jnp.where(i < nu[0], f, n_f - 1)

    ys = pl.pallas_call(
        _moe_ffn_kernel,
        grid_spec=pltpu.PrefetchScalarGridSpec(
            num_scalar_prefetch=2,
            grid=(n_tiles, n_f),
            in_specs=[pl.BlockSpec((tm, 1, d), row_map),
                      pl.BlockSpec((1, d, tf), lambda i, f, te, nu: (te[i], 0, f_idx(i, f, nu))),
                      pl.BlockSpec((1, d, tf), lambda i, f, te, nu: (te[i], 0, f_idx(i, f, nu))),
                      pl.BlockSpec((1, tf, d), lambda i, f, te, nu: (te[i], f_idx(i, f, nu), 0))],
            out_specs=pl.BlockSpec((tm, 1, d), lambda i, f, te, nu: (i, 0, 0)),
            scratch_shapes=[pltpu.VMEM((tm, d), BF16), pltpu.VMEM((tm, d), F32)]),
        out_shape=jax.ShapeDtypeStruct((n_tiles * tm, 1, d), F32),
        compiler_params=pltpu.CompilerParams(
            dimension_semantics=("arbitrary", "arbitrary"), vmem_limit_bytes=VMEM_LIMIT),
        name="moe_ffn",
    )(tile_expert, n_used, xs, w1, w3, w2)

    tc = min(COMBINE_TILE, slen)
    per_seq_c = slen // tc
    n_c = n // tc
    pos3 = pos.reshape(2 * n_c, 1, tc)
    out = pl.pallas_call(
        functools.partial(_combine_kernel, final_norm=final_norm),
        grid=(n_c,),
        in_specs=[pl.BlockSpec((1, 1, tc), lambda i: (i, 0, 0), memory_space=pltpu.SMEM),
                  pl.BlockSpec((1, 1, tc), lambda i: (i + n_c, 0, 0), memory_space=pltpu.SMEM),
                  pl.BlockSpec((tc, d), lambda i: (i, 0)),
                  pl.BlockSpec((tc, ROUTER_LANES), lambda i: (i, 0)),
                  pl.BlockSpec((1, 1, d), lambda i: (i // per_seq_c, 0, 0)),
                  pl.BlockSpec((1, d), lambda i: (0, 0)),
                  pl.BlockSpec(memory_space=pl.ANY)],
        out_specs=pl.BlockSpec((tc, d), lambda i: (i, 0)),
        out_shape=jax.ShapeDtypeStruct((n, d), F32),
        scratch_shapes=[pltpu.VMEM((2, tc, 1, d), F32), pltpu.SemaphoreType.DMA(())],
        compiler_params=pltpu.CompilerParams(
            dimension_semantics=("arbitrary",), vmem_limit_bytes=VMEM_LIMIT),
        name="moe_combine",
    )(pos3, pos3, x2, route, gate, final_w, ys)
    return out.reshape(bsz, slen, d)


def _mix_params(l, norm_mix, w_in, w_out, pool_w, pool_scale, pool_out_norm, sgu_ws, sgu_b,
                sgu_out_norm, conv_w, conv_b, lru_wa, lru_ba, lru_wx, lru_bx, lru_lambda,
                lru_out_norm, lower_bounds, hgrn_out_norm):
    tri = jnp.tril(jnp.ones((SGU_BLOCK, SGU_BLOCK), sgu_ws.dtype))
    sgu_w = jnp.concatenate([sgu_ws[l, hh] * tri for hh in range(sgu_ws.shape[1])], axis=1)
    return {
        "norm": _row(norm_mix[l]),
        "w_in": w_in[l].astype(BF16),
        "w_out": w_out[l].astype(BF16),
        "pool_w": _block_diag(pool_w[l]).astype(BF16),
        "pool_scale": _row(pool_scale[l]),
        "pool_norm": _row(pool_out_norm[l]),
        "sgu_w": sgu_w.astype(BF16),
        "sgu_b": jnp.repeat(sgu_b[l].T, HEAD_DIM, axis=1).astype(F32),
        "sgu_norm": _row(sgu_out_norm[l]),
        "conv_w": conv_w[l].astype(F32),
        "conv_b": _row(conv_b[l]),
        "lru_w": jnp.concatenate([_block_diag(lru_wa[l]), _block_diag(lru_wx[l])], axis=1).astype(BF16),
        "lru_b": _row(jnp.concatenate([lru_ba[l], lru_bx[l]])),
        "lru_log": _row(LRU_C * jax.nn.log_sigmoid(lru_lambda[l].astype(F32))),
        "lru_norm": _row(lru_out_norm[l]),
        "hgrn_lb": _row(lower_bounds[l]),
        "hgrn_norm": _row(hgrn_out_norm[l]),
    }


def kernel(x, c, w_mod, b_mod, norm_mix, norm_ffn, norm_final, w_in, w_out, pool_w, pool_scale, pool_out_norm, sgu_ws, sgu_b, sgu_out_norm, conv_w, conv_b, lru_wa, lru_ba, lru_wx, lru_bx, lru_lambda, lru_out_norm, hgrn_gamma, hgrn_out_norm, ffn_w1, ffn_w3, ffn_w2, router_w, router_b, moe_w1, moe_w3, moe_w2):
    depth = w_mod.shape[0]
    bsz, slen, d = x.shape
    p_layers = jax.nn.softmax(hgrn_gamma.astype(F32), axis=0)
    lower_bounds = jnp.cumsum(p_layers, axis=0) - p_layers[0:1]
    mod = _modulation(c, w_mod, b_mod)
    final_w = _row(norm_final)
    for l in range(depth):
        m = mod[l].reshape(bsz, 6, 1, d)
        sh1, sc1, g1, sh2, sc2, g2 = (m[:, i] for i in range(6))
        p = _mix_params(l, norm_mix, w_in, w_out, pool_w, pool_scale, pool_out_norm, sgu_ws,
                        sgu_b, sgu_out_norm, conv_w, conv_b, lru_wa, lru_ba, lru_wx, lru_bx,
                        lru_lambda, lru_out_norm, lower_bounds, hgrn_out_norm)
        x = _token_mix(x, sh1, sc1, g1, p)
        i = l // 2
        final_norm = l == depth - 1
        if l % 2 == 0:
            x = _ffn_mix(x, sh2, sc2, g2, _row(norm_ffn[l]), ffn_w1[i].astype(BF16),
                         ffn_w3[i].astype(BF16), ffn_w2[i].astype(BF16), final_w,
                         final_norm=final_norm)
        else:
            n_e = router_w.shape[2]
            rw = jnp.pad(router_w[i].astype(F32), ((0, 0), (0, ROUTER_LANES - n_e)))
            rb = jnp.pad(router_b[i].astype(F32), (0, ROUTER_LANES - n_e),
                         constant_values=NEG_BIG).reshape(1, ROUTER_LANES)
            x = _moe_mix(x, sh2, sc2, g2, _row(norm_ffn[l]), rw, rb, moe_w1[i].astype(BF16),
                         moe_w3[i].astype(BF16), moe_w2[i].astype(BF16), final_w,
                         final_norm=final_norm)
    return x
```

```python
import functools

import jax
import jax.numpy as jnp
from jax import lax
from jax.experimental import pallas as pl
from jax.experimental.pallas import tpu as pltpu

D_MODEL = 1024
GROUP = 256
N_IN_SLICES = 9
POOL_WINDOWS = (2, 4, 8, 16)
POOL_HIST = 16
CONV_WIDTH = 4
CONV_HIST = 8
SGU_BLOCK = 128
HEAD_DIM = 64
CHUNK = 64
LRU_C = 8.0
D_FF = 3584
N_EXPERTS = 8
EPS = 1e-6
MIN_FORGET = 1e-30
ROUTER_LANES = 128
NEG_BIG = -1e30
SUBLANES = 8

MIX_TILE = 512
FFN_TILE_M = 1024
FFN_TILE_F = 512
MOE_TILE = 1024
GATHER_CHUNK = 1024
COMBINE_TILE = 512
VMEM_LIMIT = 56 * 1024 * 1024

BF16 = jnp.bfloat16
F32 = jnp.float32


def _dot(a, b):
    return jnp.dot(a, b, preferred_element_type=F32)


def _dot_nt(a, b):
    return lax.dot_general(a, b, (((1,), (1,)), ((), ())), preferred_element_type=F32)


def _dot_tn(a, b):
    return lax.dot_general(a, b, (((0,), (0,)), ((), ())), preferred_element_type=F32)


def _gelu(x):
    return 0.5 * x * (1.0 + jnp.tanh(0.7978845608028654 * (x + 0.044715 * (x * x * x))))


def _logistic(x):
    return 1.0 / (1.0 + jnp.exp(-x))


def _sigmoid(x):
    return 0.5 * jnp.tanh(0.5 * x) + 0.5


def _silu(x):
    return x * _sigmoid(x)


def _rms(x, g):
    return x * lax.rsqrt(jnp.mean(x * x, axis=-1, keepdims=True) + EPS) * g


def _split_dot(x, m):
    hi = x.astype(BF16)
    lo = (x - hi.astype(F32)).astype(BF16)
    return _dot(hi, m) + _dot(lo, m)


def _mod_kernel(c_ref, w_ref, b_ref, o_ref):
    o_ref[0] = _dot(_silu(c_ref[...]), w_ref[0]) + b_ref[0]


def _modulation(c, w_mod, b_mod):
    depth, d, n = w_mod.shape
    bsz = c.shape[0]
    tn = 1536
    return pl.pallas_call(
        _mod_kernel,
        grid=(depth, n // tn),
        in_specs=[
            pl.BlockSpec((bsz, d), lambda l, j: (0, 0)),
            pl.BlockSpec((1, d, tn), lambda l, j: (l, 0, j)),
            pl.BlockSpec((1, 1, tn), lambda l, j: (l, 0, j)),
        ],
        out_specs=pl.BlockSpec((1, bsz, tn), lambda l, j: (l, 0, j)),
        out_shape=jax.ShapeDtypeStruct((depth, bsz, n), F32),
        compiler_params=pltpu.CompilerParams(vmem_limit_bytes=VMEM_LIMIT),
        name="adaln_modulation",
    )(c, w_mod, b_mod.reshape(depth, 1, n))


def _mix_kernel(x_ref, sh_ref, sc_ref, gate_ref, nw_ref, win_ref, wout_ref,
                poolw_ref, poolscale_ref, poolnorm_ref,
                sguw_ref, sgub_ref, sgunorm_ref,
                convw_ref, convb_ref, lruw_ref, lrub_ref, lrulog_ref, lrunorm_ref,
                lb_ref, hgrnnorm_ref,
                o_ref,
                pool_hist, conv_hist, lru_carry, hgrn_state, scan_a, scan_b):
    tile = x_ref.shape[1]
    j = pl.program_id(1)

    @pl.when(j == 0)
    def _():
        pool_hist[...] = jnp.zeros_like(pool_hist)
        conv_hist[...] = jnp.zeros_like(conv_hist)
        lru_carry[...] = jnp.zeros_like(lru_carry)
        hgrn_state[...] = jnp.zeros_like(hgrn_state)

    x = x_ref[0]
    h = _rms(x, nw_ref[...]) * (1.0 + sc_ref[0]) + sh_ref[0]
    hb = h.astype(BF16)

    def proj(i):
        return _dot(hb, win_ref[:, i * GROUP:(i + 1) * GROUP])

    lane = lax.broadcasted_iota(jnp.int32, (1, GROUP), 1)
    lane_head = lane // HEAD_DIM
    row = lax.broadcasted_iota(jnp.int32, (tile, 1), 0)

    bd_r = lax.broadcasted_iota(jnp.int32, (GROUP, GROUP), 0) // HEAD_DIM
    bd_c = lax.broadcasted_iota(jnp.int32, (GROUP, GROUP), 1) // HEAD_DIM
    bd_mask = (bd_r == bd_c).astype(F32)
    bd_ones = bd_mask.astype(BF16)

    pa = proj(0)
    ext = jnp.concatenate([pool_hist[...], pa], axis=0)
    s2 = ext + pltpu.roll(ext, 1, 0)
    s4 = s2 + pltpu.roll(s2, 2, 0)
    s8 = s4 + pltpu.roll(s4, 4, 0)
    s16 = s8 + pltpu.roll(s8, 8, 0)
    pool_hist[...] = pa[tile - POOL_HIST:, :]
    wsum = jnp.where(lane_head == 0, s2, jnp.where(lane_head == 1, s4,
                     jnp.where(lane_head == 2, s8, s16)))[POOL_HIST:, :]
    win = jnp.where(lane_head == 0, 2.0, jnp.where(lane_head == 1, 4.0,
                    jnp.where(lane_head == 2, 8.0, 16.0)))
    pos = (row + j * tile + 1).astype(F32)
    d = wsum / jnp.minimum(pos, win) - pa
    ya = _dot(d.astype(BF16), poolw_ref[...]) * poolscale_ref[...]
    ya = _rms(ya, poolnorm_ref[...])
    y = _dot(ya.astype(BF16), wout_ref[0:GROUP, :])

    u = _gelu(proj(1))
    vf = _gelu(proj(2))
    mu = jnp.mean(vf, axis=-1, keepdims=True)
    vc = vf - mu
    var = jnp.mean(vc * vc, axis=-1, keepdims=True)
    vn = vc * lax.rsqrt(var + EPS)
    zs = []
    for b in range(tile // SGU_BLOCK):
        vb = vn[b * SGU_BLOCK:(b + 1) * SGU_BLOCK, :]
        stack = jnp.concatenate(
            [jnp.where(lane_head == hh, vb, 0.0) for hh in range(GROUP // HEAD_DIM)], axis=0)
        zs.append(_dot(sguw_ref[...], stack.astype(BF16)) + sgub_ref[...])
    z = jnp.concatenate(zs, axis=0)
    yb = _rms(u * z, sgunorm_ref[...])
    y = y + _dot(yb.astype(BF16), wout_ref[GROUP:2 * GROUP, :])

    px = proj(3)
    ext = jnp.concatenate([conv_hist[...], px], axis=0)
    xc = (convb_ref[...] + convw_ref[3:4, :] * ext + convw_ref[2:3, :] * pltpu.roll(ext, 1, 0)
          + convw_ref[1:2, :] * pltpu.roll(ext, 2, 0) + convw_ref[0:1, :] * pltpu.roll(ext, 3, 0))
    xc = xc[CONV_HIST:, :]
    conv_hist[...] = px[tile - CONV_HIST:, :]
    gates = _dot(xc.astype(BF16), lruw_ref[...]) + lrub_ref[...]
    r = _sigmoid(gates[:, :GROUP])
    ig = _sigmoid(gates[:, GROUP:])
    log_a = r * lrulog_ref[...]
    a = jnp.exp(log_a)
    bb = jnp.sqrt(jnp.maximum(1.0 - jnp.exp(2.0 * log_a), 0.0)) * (ig * xc)
    sub = row % SUBLANES
    for s in (1, 2, 4):
        keep = sub >= s
        bb = jnp.where(keep, a * pltpu.roll(bb, s, 0) + bb, bb)
        a = jnp.where(keep, a * pltpu.roll(a, s, 0), a)
    scan_a[...] = a
    scan_b[...] = bb

    def lru_step(g, carry):
        r0 = pl.multiple_of(g * SUBLANES, SUBLANES)
        h8 = scan_a[pl.ds(r0, SUBLANES), :] * carry + scan_b[pl.ds(r0, SUBLANES), :]
        scan_b[pl.ds(r0, SUBLANES), :] = h8
        return h8[SUBLANES - 1:SUBLANES, :]

    carry = lax.fori_loop(0, tile // SUBLANES, lru_step, lru_carry[0:1, :])
    lru_carry[...] = jnp.broadcast_to(carry, lru_carry.shape)
    yc = _rms(_gelu(proj(4)) * scan_b[...], lrunorm_ref[...])
    y = y + _dot(yc.astype(BF16), wout_ref[2 * GROUP:3 * GROUP, :])

    lb = lb_ref[...]
    qf = _silu(proj(5))
    sig = _logistic(proj(6))
    log_f = jnp.log(jnp.maximum(lb + (1.0 - lb) * sig, MIN_FORGET))
    kk = (1.0 - lb) * (1.0 - sig)
    vv = proj(7)
    csub = row % CHUNK
    bcum = log_f
    for s in (1, 2, 4, 8, 16, 32):
        bcum = jnp.where(csub >= s, bcum + pltpu.roll(bcum, s, 0), bcum)
    t_idx = lax.broadcasted_iota(jnp.int32, (CHUNK, GROUP), 0)
    s_idx = lax.broadcasted_iota(jnp.int32, (CHUNK, GROUP), 1) % HEAD_DIM
    causal = (t_idx >= s_idx).astype(F32)
    n_heads = GROUP // HEAD_DIM
    outs = []
    state = hgrn_state[...]
    for c in range(tile // CHUNK):
        sl = slice(c * CHUNK, (c + 1) * CHUNK)
        bc = bcum[sl]
        bmid = bc[CHUNK // 2 - 1:CHUNK // 2, :]
        blast = bc[CHUNK - 1:CHUNK, :]
        qm = qf[sl] * jnp.exp(bc - bmid)
        km = kk[sl] * jnp.exp(bmid - bc)
        qe = qm * jnp.exp(bmid)
        kl = km * jnp.exp(blast - bmid)
        vcb = vv[sl].astype(BF16)
        k_bd = (jnp.concatenate([km] * n_heads, axis=0) * bd_mask).astype(BF16)
        scores = _dot_nt(qm.astype(BF16), k_bd) * causal
        v_bd = (jnp.concatenate([vv[sl]] * n_heads, axis=0) * bd_mask).astype(BF16)
        o = _dot(scores.astype(BF16), v_bd) + _dot_nt(qe.astype(BF16), state.astype(BF16))
        state = state * jnp.exp(blast) + _dot_tn(vcb, kl.astype(BF16)) * bd_mask
        outs.append(o)
    hgrn_state[...] = state
    o = jnp.concatenate(outs, axis=0)
    ms = _split_dot(o * o, bd_ones) * (1.0 / HEAD_DIM)
    o = o * lax.rsqrt(ms + EPS) * hgrnnorm_ref[...]
    yd = o * _silu(proj(8))
    y = y + _dot(yd.astype(BF16), wout_ref[3 * GROUP:4 * GROUP, :])

    o_ref[0] = x + gate_ref[0] * y


def _row(v):
    return v.reshape(1, -1).astype(F32)


def _block_diag(w):
    n, a, b = w.shape
    eye = jnp.eye(n, dtype=w.dtype)
    return (eye[:, None, :, None] * w[:, :, None, :]).reshape(n * a, n * b)


def _token_mix(x, sh, sc, gate, p):
    bsz, slen, d = x.shape
    tile = min(MIX_TILE, slen)
    full = lambda a: pl.BlockSpec(a.shape, lambda b, j: (0,) * a.ndim)
    per_batch = pl.BlockSpec((1, 1, d), lambda b, j: (b, 0, 0))
    x_spec = pl.BlockSpec((1, tile, d), lambda b, j: (b, j, 0))
    params = [p["norm"], p["w_in"], p["w_out"], p["pool_w"], p["pool_scale"], p["pool_norm"],
              p["sgu_w"], p["sgu_b"], p["sgu_norm"], p["conv_w"], p["conv_b"], p["lru_w"],
              p["lru_b"], p["lru_log"], p["lru_norm"], p["hgrn_lb"], p["hgrn_norm"]]
    return pl.pallas_call(
        _mix_kernel,
        grid=(bsz, slen // tile),
        in_specs=[x_spec, per_batch, per_batch, per_batch] + [full(a) for a in params],
        out_specs=x_spec,
        out_shape=jax.ShapeDtypeStruct(x.shape, x.dtype),
        scratch_shapes=[
            pltpu.VMEM((POOL_HIST, GROUP), F32),
            pltpu.VMEM((CONV_HIST, GROUP), F32),
            pltpu.VMEM((SUBLANES, GROUP), F32),
            pltpu.VMEM((GROUP, GROUP), F32),
            pltpu.VMEM((tile, GROUP), F32),
            pltpu.VMEM((tile, GROUP), F32),
        ],
        compiler_params=pltpu.CompilerParams(
            dimension_semantics=("arbitrary", "arbitrary"), vmem_limit_bytes=VMEM_LIMIT),
        name="token_mix",
    )(x, sh, sc, gate, *params)


def _ffn_kernel(x_ref, sh_ref, sc_ref, gate_ref, nw_ref, w1_ref, w3_ref, w2_ref, fw_ref,
                o_ref, h_scr, acc, *, final_norm):
    f = pl.program_id(1)

    @pl.when(f == 0)
    def _():
        h = _rms(x_ref[...], nw_ref[...]) * (1.0 + sc_ref[0]) + sh_ref[0]
        h_scr[...] = h.astype(BF16)
        acc[...] = jnp.zeros_like(acc)

    hb = h_scr[...]
    act = _silu(_dot(hb, w1_ref[...])) * _dot(hb, w3_ref[...])
    acc[...] += _dot(act.astype(BF16), w2_ref[...])

    @pl.when(f == pl.num_programs(1) - 1)
    def _():
        out = x_ref[...] + gate_ref[0] * acc[...]
        if final_norm:
            out = _rms(out, fw_ref[...])
        o_ref[...] = out


def _route_kernel(x_ref, sh_ref, sc_ref, nw_ref, rw_ref, rb_ref, h_ref, route_ref):
    h = _rms(x_ref[...], nw_ref[...]) * (1.0 + sc_ref[0]) + sh_ref[0]
    h_ref[...] = h.reshape(h_ref.shape)
    logits = jnp.dot(h, rw_ref[...], preferred_element_type=F32,
                     precision=lax.Precision.HIGHEST) + rb_ref[...]
    lane = lax.broadcasted_iota(jnp.int32, logits.shape, 1)
    v1 = jnp.max(logits, axis=-1, keepdims=True)
    i1 = jnp.min(jnp.where(logits == v1, lane, ROUTER_LANES), axis=-1, keepdims=True)
    rest = jnp.where(lane == i1, NEG_BIG, logits)
    v2 = jnp.max(rest, axis=-1, keepdims=True)
    i2 = jnp.min(jnp.where(rest == v2, lane, ROUTER_LANES), axis=-1, keepdims=True)
    p2 = jnp.exp(v2 - v1)
    w_top = 1.0 / (1.0 + p2)
    route_ref[...] = jnp.where(lane == 0, i1.astype(F32), jnp.where(lane == 1, i2.astype(F32),
                               jnp.where(lane == 2, w_top, jnp.where(lane == 3, p2 * w_top, 0.0))))


def _gather_kernel(pad_ref, nu_ref, p1_ref, p2_ref, h_ref, xs_ref, zeros, sem, zsem):
    i = pl.program_id(0)
    chunk = h_ref.shape[0]
    tile = zeros.shape[0]
    n_tiles = xs_ref.shape[0] // tile

    @pl.when(i == 0)
    def _():
        zeros[...] = jnp.zeros_like(zeros)
        for e in range(pad_ref.shape[0]):
            cp = pltpu.make_async_copy(zeros, xs_ref.at[pl.ds(pad_ref[e], tile)], zsem)
            cp.start()
            cp.wait()

        def tail(t, carry):
            cp = pltpu.make_async_copy(
                zeros, xs_ref.at[pl.ds(pl.multiple_of(t * tile, tile), tile)], zsem)
            cp.start()
            cp.wait()
            return carry

        lax.fori_loop(nu_ref[0], n_tiles, tail, 0)

    def issue(r, carry):
        pltpu.make_async_copy(h_ref.at[r], xs_ref.at[p1_ref[0, 0, r]], sem).start()
        pltpu.make_async_copy(h_ref.at[r], xs_ref.at[p2_ref[0, 0, r]], sem).start()
        return carry

    lax.fori_loop(0, chunk, issue, 0, unroll=8)
    for _ in range(2):
        pltpu.make_async_copy(h_ref, xs_ref.at[pl.ds(0, chunk)], sem).wait()


def _moe_ffn_kernel(te_ref, nu_ref, xs_ref, w1_ref, w3_ref, w2_ref, ys_ref, h_scr, acc):
    i = pl.program_id(0)
    f = pl.program_id(1)

    @pl.when(i < nu_ref[0])
    def _():
        @pl.when(f == 0)
        def _():
            h_scr[...] = xs_ref[:, 0, :].astype(BF16)
            acc[...] = jnp.zeros_like(acc)

        hb = h_scr[...]
        act = _silu(_dot(hb, w1_ref[0])) * _dot(hb, w3_ref[0])
        acc[...] += _dot(act.astype(BF16), w2_ref[0])

        @pl.when(f == pl.num_programs(1) - 1)
        def _():
            ys_ref[...] = acc[...].reshape(ys_ref.shape)

    @pl.when((i >= nu_ref[0]) & (f == 0))
    def _():
        ys_ref[...] = jnp.zeros_like(ys_ref)


def _combine_kernel(p1_ref, p2_ref, x_ref, route_ref, gate_ref, fw_ref, ys_ref, o_ref,
                    ybuf, sem, *, final_norm):
    tm = x_ref.shape[0]

    def issue(r, carry):
        pltpu.make_async_copy(ys_ref.at[p1_ref[0, 0, r]], ybuf.at[0, r], sem).start()
        pltpu.make_async_copy(ys_ref.at[p2_ref[0, 0, r]], ybuf.at[1, r], sem).start()
        return carry

    lax.fori_loop(0, tm, issue, 0, unroll=8)
    for s in range(2):
        pltpu.make_async_copy(ys_ref.at[pl.ds(0, tm)], ybuf.at[s], sem).wait()
    route = route_ref[...]
    y = route[:, 2:3] * ybuf[0, :, 0, :] + route[:, 3:4] * ybuf[1, :, 0, :]
    out = x_ref[...] + gate_ref[0] * y
    if final_norm:
        out = _rms(out, fw_ref[...])
    o_ref[...] = out


def _ffn_mix(x, sh, sc, gate, norm_w, w1, w3, w2, final_w, *, final_norm):
    bsz, slen, d = x.shape
    tm = min(FFN_TILE_M, slen)
    per_seq = slen // tm
    d_ff = w1.shape[1]
    tf = min(FFN_TILE_F, d_ff)
    x2 = x.reshape(bsz * slen, d)
    x_spec = pl.BlockSpec((tm, d), lambda i, f: (i, 0))
    per_batch = pl.BlockSpec((1, 1, d), lambda i, f: (i // per_seq, 0, 0))
    row_spec = pl.BlockSpec((1, d), lambda i, f: (0, 0))
    out = pl.pallas_call(
        functools.partial(_ffn_kernel, final_norm=final_norm),
        grid=(bsz * per_seq, d_ff // tf),
        in_specs=[x_spec, per_batch, per_batch, per_batch, row_spec,
                  pl.BlockSpec((d, tf), lambda i, f: (0, f)),
                  pl.BlockSpec((d, tf), lambda i, f: (0, f)),
                  pl.BlockSpec((tf, d), lambda i, f: (f, 0)),
                  row_spec],
        out_specs=x_spec,
        out_shape=jax.ShapeDtypeStruct(x2.shape, x2.dtype),
        scratch_shapes=[pltpu.VMEM((tm, d), BF16), pltpu.VMEM((tm, d), F32)],
        compiler_params=pltpu.CompilerParams(
            dimension_semantics=("arbitrary", "arbitrary"), vmem_limit_bytes=VMEM_LIMIT),
        name="ffn_mix",
    )(x2, sh, sc, gate, norm_w, w1, w3, w2, final_w)
    return out.reshape(bsz, slen, d)


def _moe_mix(x, sh, sc, gate, norm_w, rw, rb, w1, w3, w2, final_w, *, final_norm):
    bsz, slen, d = x.shape
    n = bsz * slen
    n_e, _, d_ff = w1.shape
    x2 = x.reshape(n, d)

    tr = min(FFN_TILE_M, slen)
    per_seq = slen // tr
    per_batch = pl.BlockSpec((1, 1, d), lambda i: (i // per_seq, 0, 0))
    h3, route = pl.pallas_call(
        _route_kernel,
        grid=(n // tr,),
        in_specs=[pl.BlockSpec((tr, d), lambda i: (i, 0)), per_batch, per_batch,
                  pl.BlockSpec((1, d), lambda i: (0, 0)),
                  pl.BlockSpec(rw.shape, lambda i: (0, 0)),
                  pl.BlockSpec(rb.shape, lambda i: (0, 0))],
        out_specs=[pl.BlockSpec((tr, 1, d), lambda i: (i, 0, 0)),
                   pl.BlockSpec((tr, ROUTER_LANES), lambda i: (i, 0))],
        out_shape=[jax.ShapeDtypeStruct((n, 1, d), F32),
                   jax.ShapeDtypeStruct((n, ROUTER_LANES), F32)],
        compiler_params=pltpu.CompilerParams(vmem_limit_bytes=VMEM_LIMIT),
        name="moe_route",
    )(x2, sh, sc, norm_w, rw, rb)

    tm = min(MOE_TILE, n)
    n_tiles = 2 * n // tm + n_e
    rows = (n_tiles + 1) * tm
    e_flat = jnp.concatenate([route[:, 0], route[:, 1]]).astype(jnp.int32)
    onehot = (e_flat[:, None] == jnp.arange(n_e, dtype=jnp.int32)[None, :]).astype(jnp.int32)
    csum = jnp.cumsum(onehot, axis=0)
    counts = csum[-1]
    padded = ((counts + tm - 1) // tm) * tm
    gstart = jnp.cumsum(padded) - padded
    pos = jnp.sum(onehot * (gstart[None, :] + csum - 1), axis=1).astype(jnp.int32)
    pad_start = (gstart + counts).astype(jnp.int32)
    tile_end = jnp.cumsum(padded // tm)
    n_used = tile_end[-1:].astype(jnp.int32)
    tile_ids = jnp.minimum(jnp.arange(n_tiles, dtype=jnp.int32), n_used[0] - 1)
    tile_expert = jnp.sum((tile_ids[:, None] >= tile_end[None, :]).astype(jnp.int32), axis=1)

    chunk = min(GATHER_CHUNK, n)
    per_slot = n // chunk
    xs = pl.pallas_call(
        _gather_kernel,
        grid_spec=pltpu.PrefetchScalarGridSpec(
            num_scalar_prefetch=2,
            grid=(per_slot,),
            in_specs=[pl.BlockSpec((1, 1, chunk), lambda i, pad, nu: (i, 0, 0),
                                   memory_space=pltpu.SMEM),
                      pl.BlockSpec((1, 1, chunk), lambda i, pad, nu: (i + per_slot, 0, 0),
                                   memory_space=pltpu.SMEM),
                      pl.BlockSpec((chunk, 1, d), lambda i, pad, nu: (i, 0, 0))],
            out_specs=pl.BlockSpec(memory_space=pl.ANY),
            scratch_shapes=[pltpu.VMEM((tm, 1, d), F32), pltpu.SemaphoreType.DMA(()),
                            pltpu.SemaphoreType.DMA(())]),
        out_shape=jax.ShapeDtypeStruct((rows, 1, d), F32),
        compiler_params=pltpu.CompilerParams(
            dimension_semantics=("arbitrary",), vmem_limit_bytes=VMEM_LIMIT),
        name="moe_gather",
    )(pad_start, n_used, pos.reshape(2 * per_slot, 1, chunk), pos.reshape(2 * per_slot, 1, chunk), h3)

    tf = min(FFN_TILE_F, d_ff)
    n_f = d_ff // tf

    def row_map(i, f, te, nu):
        return (jnp.minimum(i, nu[0] - 1), 0, 0)

    def f_idx(i, f, nu):
        return jnp.where(i < nu[0], f, n_f - 1)

    ys = pl.pallas_call(
        _moe_ffn_kernel,
        grid_spec=pltpu.PrefetchScalarGridSpec(
            num_scalar_prefetch=2,
            grid=(n_tiles, n_f),
            in_specs=[pl.BlockSpec((tm, 1, d), row_map),
                      pl.BlockSpec((1, d, tf), lambda i, f, te, nu: (te[i], 0, f_idx(i, f, nu))),
                      pl.BlockSpec((1, d, tf), lambda i, f, te, nu: (te[i], 0, f_idx(i, f, nu))),
                      pl.BlockSpec((1, tf, d), lambda i, f, te, nu: (te[i], f_idx(i, f, nu), 0))],
            out_specs=pl.BlockSpec((tm, 1, d), lambda i, f, te, nu: (i, 0, 0)),
            scratch_shapes=[pltpu.VMEM((tm, d), BF16), pltpu.VMEM((tm, d), F32)]),
        out_shape=jax.ShapeDtypeStruct((n_tiles * tm, 1, d), F32),
        compiler_params=pltpu.CompilerParams(
            dimension_semantics=("arbitrary", "arbitrary"), vmem_limit_bytes=VMEM_LIMIT),
        name="moe_ffn",
    )(tile_expert, n_used, xs, w1, w3, w2)

    tc = min(COMBINE_TILE, slen)
    per_seq_c = slen // tc
    n_c = n // tc
    pos3 = pos.reshape(2 * n_c, 1, tc)
    out = pl.pallas_call(
        functools.partial(_combine_kernel, final_norm=final_norm),
        grid=(n_c,),
        in_specs=[pl.BlockSpec((1, 1, tc), lambda i: (i, 0, 0), memory_space=pltpu.SMEM),
                  pl.BlockSpec((1, 1, tc), lambda i: (i + n_c, 0, 0), memory_space=pltpu.SMEM),
                  pl.BlockSpec((tc, d), lambda i: (i, 0)),
                  pl.BlockSpec((tc, ROUTER_LANES), lambda i: (i, 0)),
                  pl.BlockSpec((1, 1, d), lambda i: (i // per_seq_c, 0, 0)),
                  pl.BlockSpec((1, d), lambda i: (0, 0)),
                  pl.BlockSpec(memory_space=pl.ANY)],
        out_specs=pl.BlockSpec((tc, d), lambda i: (i, 0)),
        out_shape=jax.ShapeDtypeStruct((n, d), F32),
        scratch_shapes=[pltpu.VMEM((2, tc, 1, d), F32), pltpu.SemaphoreType.DMA(())],
        compiler_params=pltpu.CompilerParams(
            dimension_semantics=("arbitrary",), vmem_limit_bytes=VMEM_LIMIT),
        name="moe_combine",
    )(pos3, pos3, x2, route, gate, final_w, ys)
    return out.reshape(bsz, slen, d)


def _mix_params(l, norm_mix, w_in, w_out, pool_w, pool_scale, pool_out_norm, sgu_ws, sgu_b,
                sgu_out_norm, conv_w, conv_b, lru_wa, lru_ba, lru_wx, lru_bx, lru_lambda,
                lru_out_norm, lower_bounds, hgrn_out_norm):
    tri = jnp.tril(jnp.ones((SGU_BLOCK, SGU_BLOCK), sgu_ws.dtype))
    sgu_w = jnp.concatenate([sgu_ws[l, hh] * tri for hh in range(sgu_ws.shape[1])], axis=1)
    return {
        "norm": _row(norm_mix[l]),
        "w_in": w_in[l].astype(BF16),
        "w_out": w_out[l].astype(BF16),
        "pool_w": _block_diag(pool_w[l]).astype(BF16),
        "pool_scale": _row(pool_scale[l]),
        "pool_norm": _row(pool_out_norm[l]),
        "sgu_w": sgu_w.astype(BF16),
        "sgu_b": jnp.repeat(sgu_b[l].T, HEAD_DIM, axis=1).astype(F32),
        "sgu_norm": _row(sgu_out_norm[l]),
        "conv_w": conv_w[l].astype(F32),
        "conv_b": _row(conv_b[l]),
        "lru_w": jnp.concatenate([_block_diag(lru_wa[l]), _block_diag(lru_wx[l])], axis=1).astype(BF16),
        "lru_b": _row(jnp.concatenate([lru_ba[l], lru_bx[l]])),
        "lru_log": _row(LRU_C * jax.nn.log_sigmoid(lru_lambda[l].astype(F32))),
        "lru_norm": _row(lru_out_norm[l]),
        "hgrn_lb": _row(lower_bounds[l]),
        "hgrn_norm": _row(hgrn_out_norm[l]),
    }


def kernel(x, c, w_mod, b_mod, norm_mix, norm_ffn, norm_final, w_in, w_out, pool_w, pool_scale, pool_out_norm, sgu_ws, sgu_b, sgu_out_norm, conv_w, conv_b, lru_wa, lru_ba, lru_wx, lru_bx, lru_lambda, lru_out_norm, hgrn_gamma, hgrn_out_norm, ffn_w1, ffn_w3, ffn_w2, router_w, router_b, moe_w1, moe_w3, moe_w2):
    depth = w_mod.shape[0]
    bsz, slen, d = x.shape
    p_layers = jax.nn.softmax(hgrn_gamma.astype(F32), axis=0)
    lower_bounds = jnp.cumsum(p_layers, axis=0) - p_layers[0:1]
    mod = _modulation(c, w_mod, b_mod)
    final_w = _row(norm_final)
    for l in range(depth):
        m = mod[l].reshape(bsz, 6, 1, d)
        sh1, sc1, g1, sh2, sc2, g2 = (m[:, i] for i in range(6))
        p = _mix_params(l, norm_mix, w_in, w_out, pool_w, pool_scale, pool_out_norm, sgu_ws,
                        sgu_b, sgu_out_norm, conv_w, conv_b, lru_wa, lru_ba, lru_wx, lru_bx,
                        lru_lambda, lru_out_norm, lower_bounds, hgrn_out_norm)
        x = _token_mix(x, sh1, sc1, g1, p)
        i = l // 2
        final_norm = l == depth - 1
        if l % 2 == 0:
            x = _ffn_mix(x, sh2, sc2, g2, _row(norm_ffn[l]), ffn_w1[i].astype(BF16),
                         ffn_w3[i].astype(BF16), ffn_w2[i].astype(BF16), final_w,
                         final_norm=final_norm)
        else:
            n_e = router_w.shape[2]
            rw = jnp.pad(router_w[i].astype(F32), ((0, 0), (0, ROUTER_LANES - n_e)))
            rb = jnp.pad(router_b[i].astype(F32), (0, ROUTER_LANES - n_e),
                         constant_values=NEG_BIG).reshape(1, ROUTER_LANES)
            x = _moe_mix(x, sh2, sc2, g2, _row(norm_ffn[l]), rw, rb, moe_w1[i].astype(BF16),
                         moe_w3[i].astype(BF16), moe_w2[i].astype(BF16), final_w,
                         final_norm=final_norm)
    return x
```

```python
import functools

import jax
import jax.numpy as jnp
from jax import lax
from jax.experimental import pallas as pl
from jax.experimental.pallas import tpu as pltpu

D_MODEL = 1024
GROUP = 256
N_IN_SLICES = 9
POOL_WINDOWS = (2, 4, 8, 16)
POOL_HIST = 16
CONV_WIDTH = 4
CONV_HIST = 8
SGU_BLOCK = 128
HEAD_DIM = 64
CHUNK = 64
LRU_C = 8.0
D_FF = 3584
N_EXPERTS = 8
EPS = 1e-6
MIN_FORGET = 1e-30
ROUTER_LANES = 128
NEG_BIG = -1e30
SUBLANES = 8

MIX_TILE = 512
FFN_TILE_M = 1024
FFN_TILE_F = 512
MOE_TILE = 1024
GATHER_CHUNK = 1024
COMBINE_TILE = 512
VMEM_LIMIT = 56 * 1024 * 1024

BF16 = jnp.bfloat16
F32 = jnp.float32


def _dot(a, b):
    return jnp.dot(a, b, preferred_element_type=F32)


def _dot_nt(a, b):
    return lax.dot_general(a, b, (((1,), (1,)), ((), ())), preferred_element_type=F32)


def _dot_tn(a, b):
    return lax.dot_general(a, b, (((0,), (0,)), ((), ())), preferred_element_type=F32)


def _gelu(x):
    return 0.5 * x * (1.0 + jnp.tanh(0.7978845608028654 * (x + 0.044715 * (x * x * x))))


def _logistic(x):
    return 1.0 / (1.0 + jnp.exp(-x))


def _sigmoid(x):
    return 0.5 * jnp.tanh(0.5 * x) + 0.5


def _silu(x):
    return x * _sigmoid(x)


def _rms(x, g):
    return x * lax.rsqrt(jnp.mean(x * x, axis=-1, keepdims=True) + EPS) * g


def _split_dot(x, m):
    hi = x.astype(BF16)
    lo = (x - hi.astype(F32)).astype(BF16)
    return _dot(hi, m) + _dot(lo, m)


def _mod_kernel(c_ref, w_ref, b_ref, o_ref):
    o_ref[0] = _dot(_silu(c_ref[...]), w_ref[0]) + b_ref[0]


def _modulation(c, w_mod, b_mod):
    depth, d, n = w_mod.shape
    bsz = c.shape[0]
    tn = 1536
    return pl.pallas_call(
        _mod_kernel,
        grid=(depth, n // tn),
        in_specs=[
            pl.BlockSpec((bsz, d), lambda l, j: (0, 0)),
            pl.BlockSpec((1, d, tn), lambda l, j: (l, 0, j)),
            pl.BlockSpec((1, 1, tn), lambda l, j: (l, 0, j)),
        ],
        out_specs=pl.BlockSpec((1, bsz, tn), lambda l, j: (l, 0, j)),
        out_shape=jax.ShapeDtypeStruct((depth, bsz, n), F32),
        compiler_params=pltpu.CompilerParams(vmem_limit_bytes=VMEM_LIMIT),
        name="adaln_modulation",
    )(c, w_mod, b_mod.reshape(depth, 1, n))


def _mix_kernel(x_ref, sh_ref, sc_ref, gate_ref, nw_ref, win_ref, wout_ref,
                poolw_ref, poolscale_ref, poolnorm_ref,
                sguw_ref, sgub_ref, sgunorm_ref,
                convw_ref, convb_ref, lruw_ref, lrub_ref, lrulog_ref, lrunorm_ref,
                lb_ref, hgrnnorm_ref,
                o_ref,
                pool_hist, conv_hist, lru_carry, hgrn_state, scan_a, scan_b, proj_scr):
    tile = x_ref.shape[1]
    j = pl.program_id(1)

    @pl.when(j == 0)
    def _():
        pool_hist[...] = jnp.zeros_like(pool_hist)
        conv_hist[...] = jnp.zeros_like(conv_hist)
        lru_carry[...] = jnp.zeros_like(lru_carry)
        hgrn_state[...] = jnp.zeros_like(hgrn_state)

    x = x_ref[0]
    h = _rms(x, nw_ref[...]) * (1.0 + sc_ref[0]) + sh_ref[0]
    hb = h.astype(BF16)

    proj_scr[...] = _dot(hb, win_ref[...])

    def proj(i):
        return proj_scr[:, i * GROUP:(i + 1) * GROUP]

    lane = lax.broadcasted_iota(jnp.int32, (1, GROUP), 1)
    lane_head = lane // HEAD_DIM
    row = lax.broadcasted_iota(jnp.int32, (tile, 1), 0)

    bd_r = lax.broadcasted_iota(jnp.int32, (GROUP, GROUP), 0) // HEAD_DIM
    bd_c = lax.broadcasted_iota(jnp.int32, (GROUP, GROUP), 1) // HEAD_DIM
    bd_mask = (bd_r == bd_c).astype(F32)
    bd_ones = bd_mask.astype(BF16)

    pa = proj(0)
    ext = jnp.concatenate([pool_hist[...], pa], axis=0)
    s2 = ext + pltpu.roll(ext, 1, 0)
    s4 = s2 + pltpu.roll(s2, 2, 0)
    s8 = s4 + pltpu.roll(s4, 4, 0)
    s16 = s8 + pltpu.roll(s8, 8, 0)
    pool_hist[...] = pa[tile - POOL_HIST:, :]
    wsum = jnp.where(lane_head == 0, s2, jnp.where(lane_head == 1, s4,
                     jnp.where(lane_head == 2, s8, s16)))[POOL_HIST:, :]
    win = jnp.where(lane_head == 0, 2.0, jnp.where(lane_head == 1, 4.0,
                    jnp.where(lane_head == 2, 8.0, 16.0)))
    pos = (row + j * tile + 1).astype(F32)
    d = wsum / jnp.minimum(pos, win) - pa
    ya = _dot(d.astype(BF16), poolw_ref[...]) * poolscale_ref[...]
    ya = _rms(ya, poolnorm_ref[...])

    u = _gelu(proj(1))
    vf = _gelu(proj(2))
    mu = jnp.mean(vf, axis=-1, keepdims=True)
    vc = vf - mu
    var = jnp.mean(vc * vc, axis=-1, keepdims=True)
    vn = vc * lax.rsqrt(var + EPS)
    zs = []
    for b in range(tile // SGU_BLOCK):
        vb = vn[b * SGU_BLOCK:(b + 1) * SGU_BLOCK, :]
        stack = jnp.concatenate(
            [jnp.where(lane_head == hh, vb, 0.0) for hh in range(GROUP // HEAD_DIM)], axis=0)
        zs.append(_dot(sguw_ref[...], stack.astype(BF16)) + sgub_ref[...])
    z = jnp.concatenate(zs, axis=0)
    yb = _rms(u * z, sgunorm_ref[...])

    px = proj(3)
    ext = jnp.concatenate([conv_hist[...], px], axis=0)
    xc = (convb_ref[...] + convw_ref[3:4, :] * ext + convw_ref[2:3, :] * pltpu.roll(ext, 1, 0)
          + convw_ref[1:2, :] * pltpu.roll(ext, 2, 0) + convw_ref[0:1, :] * pltpu.roll(ext, 3, 0))
    xc = xc[CONV_HIST:, :]
    conv_hist[...] = px[tile - CONV_HIST:, :]
    gates = _dot(xc.astype(BF16), lruw_ref[...]) + lrub_ref[...]
    r = _sigmoid(gates[:, :GROUP])
    ig = _sigmoid(gates[:, GROUP:])
    log_a = r * lrulog_ref[...]
    a = jnp.exp(log_a)
    bb = jnp.sqrt(jnp.maximum(1.0 - jnp.exp(2.0 * log_a), 0.0)) * (ig * xc)
    a3 = a.reshape(tile // SUBLANES, SUBLANES, GROUP)
    b3 = bb.reshape(tile // SUBLANES, SUBLANES, GROUP)
    sub = lax.broadcasted_iota(jnp.int32, (1, SUBLANES, 1), 1)
    for s in (1, 2, 4):
        keep = sub >= s
        b3 = jnp.where(keep, a3 * pltpu.roll(b3, s, 1) + b3, b3)
        a3 = jnp.where(keep, a3 * pltpu.roll(a3, s, 1), a3)
    scan_a[...] = a3.reshape(tile, GROUP)
    scan_b[...] = b3.reshape(tile, GROUP)

    def lru_step(g, carry):
        r0 = pl.multiple_of(g * SUBLANES, SUBLANES)
        h8 = scan_a[pl.ds(r0, SUBLANES), :] * carry + scan_b[pl.ds(r0, SUBLANES), :]
        scan_b[pl.ds(r0, SUBLANES), :] = h8
        return h8[SUBLANES - 1:SUBLANES, :]

    carry = lax.fori_loop(0, tile // SUBLANES, lru_step, lru_carry[0:1, :], unroll=True)
    lru_carry[...] = jnp.broadcast_to(carry, lru_carry.shape)
    yc = _rms(_gelu(proj(4)) * scan_b[...], lrunorm_ref[...])

    lb = lb_ref[...]
    qf = _silu(proj(5))
    sig = _logistic(proj(6))
    log_f = jnp.log(jnp.maximum(lb + (1.0 - lb) * sig, MIN_FORGET))
    kk = (1.0 - lb) * (1.0 - sig)
    vv = proj(7)
    csub = row % CHUNK
    bcum = log_f
    for s in (1, 2, 4, 8, 16, 32):
        bcum = jnp.where(csub >= s, bcum + pltpu.roll(bcum, s, 0), bcum)
    t_idx = lax.broadcasted_iota(jnp.int32, (CHUNK, GROUP), 0)
    s_idx = lax.broadcasted_iota(jnp.int32, (CHUNK, GROUP), 1) % HEAD_DIM
    causal = (t_idx >= s_idx).astype(F32)
    n_heads = GROUP // HEAD_DIM
    outs = []
    state = hgrn_state[...]
    for c in range(tile // CHUNK):
        sl = slice(c * CHUNK, (c + 1) * CHUNK)
        bc = bcum[sl]
        bmid = bc[CHUNK // 2 - 1:CHUNK // 2, :]
        blast = bc[CHUNK - 1:CHUNK, :]
        qm = qf[sl] * jnp.exp(bc - bmid)
        km = kk[sl] * jnp.exp(bmid - bc)
        qe = qm * jnp.exp(bmid)
        kl = km * jnp.exp(blast - bmid)
        vcb = vv[sl].astype(BF16)
        k_bd = (jnp.concatenate([km] * n_heads, axis=0) * bd_mask).astype(BF16)
        scores = _dot_nt(qm.astype(BF16), k_bd) * causal
        v_bd = (jnp.concatenate([vv[sl]] * n_heads, axis=0) * bd_mask).astype(BF16)
        o = _dot(scores.astype(BF16), v_bd) + _dot_nt(qe.astype(BF16), state.astype(BF16))
        state = state * jnp.exp(blast) + _dot_tn(vcb, kl.astype(BF16)) * bd_mask
        outs.append(o)
    hgrn_state[...] = state
    o = jnp.concatenate(outs, axis=0)
    ms = _split_dot(o * o, bd_ones) * (1.0 / HEAD_DIM)
    o = o * lax.rsqrt(ms + EPS) * hgrnnorm_ref[...]
    yd = o * _silu(proj(8))
    ycat = jnp.concatenate([ya.astype(BF16), yb.astype(BF16), yc.astype(BF16), yd.astype(BF16)],
                           axis=1)
    y = _dot(ycat, wout_ref[...])

    o_ref[0] = x + gate_ref[0] * y


def _row(v):
    return v.reshape(1, -1).astype(F32)


def _block_diag(w):
    n, a, b = w.shape
    eye = jnp.eye(n, dtype=w.dtype)
    return (eye[:, None, :, None] * w[:, :, None, :]).reshape(n * a, n * b)


def _token_mix(x, sh, sc, gate, p):
    bsz, slen, d = x.shape
    tile = min(MIX_TILE, slen)
    full = lambda a: pl.BlockSpec(a.shape, lambda b, j: (0,) * a.ndim)
    per_batch = pl.BlockSpec((1, 1, d), lambda b, j: (b, 0, 0))
    x_spec = pl.BlockSpec((1, tile, d), lambda b, j: (b, j, 0))
    params = [p["norm"], p["w_in"], p["w_out"], p["pool_w"], p["pool_scale"], p["pool_norm"],
              p["sgu_w"], p["sgu_b"], p["sgu_norm"], p["conv_w"], p["conv_b"], p["lru_w"],
              p["lru_b"], p["lru_log"], p["lru_norm"], p["hgrn_lb"], p["hgrn_norm"]]
    return pl.pallas_call(
        _mix_kernel,
        grid=(bsz, slen // tile),
        in_specs=[x_spec, per_batch, per_batch, per_batch] + [full(a) for a in params],
        out_specs=x_spec,
        out_shape=jax.ShapeDtypeStruct(x.shape, x.dtype),
        scratch_shapes=[
            pltpu.VMEM((POOL_HIST, GROUP), F32),
            pltpu.VMEM((CONV_HIST, GROUP), F32),
            pltpu.VMEM((SUBLANES, GROUP), F32),
            pltpu.VMEM((GROUP, GROUP), F32),
            pltpu.VMEM((tile, GROUP), F32),
            pltpu.VMEM((tile, GROUP), F32),
            pltpu.VMEM((tile, N_IN_SLICES * GROUP), F32),
        ],
        compiler_params=pltpu.CompilerParams(
            dimension_semantics=("arbitrary", "arbitrary"), vmem_limit_bytes=VMEM_LIMIT),
        name="token_mix",
    )(x, sh, sc, gate, *params)


def _ffn_kernel(x_ref, sh_ref, sc_ref, gate_ref, nw_ref, w1_ref, w3_ref, w2_ref, fw_ref,
                o_ref, h_scr, acc, *, final_norm):
    f = pl.program_id(1)

    @pl.when(f == 0)
    def _():
        h = _rms(x_ref[...], nw_ref[...]) * (1.0 + sc_ref[0]) + sh_ref[0]
        h_scr[...] = h.astype(BF16)
        acc[...] = jnp.zeros_like(acc)

    hb = h_scr[...]
    act = _silu(_dot(hb, w1_ref[...])) * _dot(hb, w3_ref[...])
    acc[...] += _dot(act.astype(BF16), w2_ref[...])

    @pl.when(f == pl.num_programs(1) - 1)
    def _():
        out = x_ref[...] + gate_ref[0] * acc[...]
        if final_norm:
            out = _rms(out, fw_ref[...])
        o_ref[...] = out


def _route_kernel(x_ref, sh_ref, sc_ref, nw_ref, rw_ref, rb_ref, h_ref, route_ref):
    h = _rms(x_ref[...], nw_ref[...]) * (1.0 + sc_ref[0]) + sh_ref[0]
    h_ref[...] = h.reshape(h_ref.shape)
    logits = jnp.dot(h, rw_ref[...], preferred_element_type=F32,
                     precision=lax.Precision.HIGHEST) + rb_ref[...]
    lane = lax.broadcasted_iota(jnp.int32, logits.shape, 1)
    v1 = jnp.max(logits, axis=-1, keepdims=True)
    i1 = jnp.min(jnp.where(logits == v1, lane, ROUTER_LANES), axis=-1, keepdims=True)
    rest = jnp.where(lane == i1, NEG_BIG, logits)
    v2 = jnp.max(rest, axis=-1, keepdims=True)
    i2 = jnp.min(jnp.where(rest == v2, lane, ROUTER_LANES), axis=-1, keepdims=True)
    p2 = jnp.exp(v2 - v1)
    w_top = 1.0 / (1.0 + p2)
    route_ref[...] = jnp.where(lane == 0, i1.astype(F32), jnp.where(lane == 1, i2.astype(F32),
                               jnp.where(lane == 2, w_top, jnp.where(lane == 3, p2 * w_top, 0.0))))


def _gather_kernel(pad_ref, nu_ref, p1_ref, p2_ref, h_ref, xs_ref, zeros, sem, zsem):
    i = pl.program_id(0)
    chunk = h_ref.shape[0]
    tile = zeros.shape[0]
    n_tiles = xs_ref.shape[0] // tile

    @pl.when(i == 0)
    def _():
        zeros[...] = jnp.zeros_like(zeros)
        for e in range(pad_ref.shape[0]):
            cp = pltpu.make_async_copy(zeros, xs_ref.at[pl.ds(pad_ref[e], tile)], zsem)
            cp.start()
            cp.wait()

        def tail(t, carry):
            cp = pltpu.make_async_copy(
                zeros, xs_ref.at[pl.ds(pl.multiple_of(t * tile, tile), tile)], zsem)
            cp.start()
            cp.wait()
            return carry

        lax.fori_loop(nu_ref[0], n_tiles, tail, 0)

    def issue(r, carry):
        pltpu.make_async_copy(h_ref.at[r], xs_ref.at[p1_ref[0, 0, r]], sem).start()
        pltpu.make_async_copy(h_ref.at[r], xs_ref.at[p2_ref[0, 0, r]], sem).start()
        return carry

    lax.fori_loop(0, chunk, issue, 0, unroll=8)
    for _ in range(2):
        pltpu.make_async_copy(h_ref, xs_ref.at[pl.ds(0, chunk)], sem).wait()


def _moe_ffn_kernel(te_ref, nu_ref, xs_ref, w1_ref, w3_ref, w2_ref, ys_ref, h_scr, acc):
    i = pl.program_id(0)
    f = pl.program_id(1)

    @pl.when(i < nu_ref[0])
    def _():
        @pl.when(f == 0)
        def _():
            h_scr[...] = xs_ref[:, 0, :].astype(BF16)
            acc[...] = jnp.zeros_like(acc)

        hb = h_scr[...]
        act = _silu(_dot(hb, w1_ref[0])) * _dot(hb, w3_ref[0])
        acc[...] += _dot(act.astype(BF16), w2_ref[0])

        @pl.when(f == pl.num_programs(1) - 1)
        def _():
            ys_ref[...] = acc[...].reshape(ys_ref.shape)

    @pl.when((i >= nu_ref[0]) & (f == 0))
    def _():
        ys_ref[...] = jnp.zeros_like(ys_ref)


def _combine_kernel(p1_ref, p2_ref, x_ref, route_ref, gate_ref, fw_ref, ys_ref, o_ref,
                    ybuf, sem, *, final_norm):
    tm = x_ref.shape[0]

    def issue(r, carry):
        pltpu.make_async_copy(ys_ref.at[p1_ref[0, 0, r]], ybuf.at[0, r], sem).start()
        pltpu.make_async_copy(ys_ref.at[p2_ref[0, 0, r]], ybuf.at[1, r], sem).start()
        return carry

    lax.fori_loop(0, tm, issue, 0, unroll=8)
    for s in range(2):
        pltpu.make_async_copy(ys_ref.at[pl.ds(0, tm)], ybuf.at[s], sem).wait()
    route = route_ref[...]
    y = route[:, 2:3] * ybuf[0, :, 0, :] + route[:, 3:4] * ybuf[1, :, 0, :]
    out = x_ref[...] + gate_ref[0] * y
    if final_norm:
        out = _rms(out, fw_ref[...])
    o_ref[...] = out


def _ffn_mix(x, sh, sc, gate, norm_w, w1, w3, w2, final_w, *, final_norm):
    bsz, slen, d = x.shape
    tm = min(FFN_TILE_M, slen)
    per_seq = slen // tm
    d_ff = w1.shape[1]
    tf = min(FFN_TILE_F, d_ff)
    x2 = x.reshape(bsz * slen, d)
    x_spec = pl.BlockSpec((tm, d), lambda i, f: (i, 0))
    per_batch = pl.BlockSpec((1, 1, d), lambda i, f: (i // per_seq, 0, 0))
    row_spec = pl.BlockSpec((1, d), lambda i, f: (0, 0))
    out = pl.pallas_call(
        functools.partial(_ffn_kernel, final_norm=final_norm),
        grid=(bsz * per_seq, d_ff // tf),
        in_specs=[x_spec, per_batch, per_batch, per_batch, row_spec,
                  pl.BlockSpec((d, tf), lambda i, f: (0, f)),
                  pl.BlockSpec((d, tf), lambda i, f: (0, f)),
                  pl.BlockSpec((tf, d), lambda i, f: (f, 0)),
                  row_spec],
        out_specs=x_spec,
        out_shape=jax.ShapeDtypeStruct(x2.shape, x2.dtype),
        scratch_shapes=[pltpu.VMEM((tm, d), BF16), pltpu.VMEM((tm, d), F32)],
        compiler_params=pltpu.CompilerParams(
            dimension_semantics=("arbitrary", "arbitrary"), vmem_limit_bytes=VMEM_LIMIT),
        name="ffn_mix",
    )(x2, sh, sc, gate, norm_w, w1, w3, w2, final_w)
    return out.reshape(bsz, slen, d)


def _moe_mix(x, sh, sc, gate, norm_w, rw, rb, w1, w3, w2, final_w, *, final_norm):
    bsz, slen, d = x.shape
    n = bsz * slen
    n_e, _, d_ff = w1.shape
    x2 = x.reshape(n, d)

    tr = min(FFN_TILE_M, slen)
    per_seq = slen // tr
    per_batch = pl.BlockSpec((1, 1, d), lambda i: (i // per_seq, 0, 0))
    h3, route = pl.pallas_call(
        _route_kernel,
        grid=(n // tr,),
        in_specs=[pl.BlockSpec((tr, d), lambda i: (i, 0)), per_batch, per_batch,
                  pl.BlockSpec((1, d), lambda i: (0, 0)),
                  pl.BlockSpec(rw.shape, lambda i: (0, 0)),
                  pl.BlockSpec(rb.shape, lambda i: (0, 0))],
        out_specs=[pl.BlockSpec((tr, 1, d), lambda i: (i, 0, 0)),
                   pl.BlockSpec((tr, ROUTER_LANES), lambda i: (i, 0))],
        out_shape=[jax.ShapeDtypeStruct((n, 1, d), F32),
                   jax.ShapeDtypeStruct((n, ROUTER_LANES), F32)],
        compiler_params=pltpu.CompilerParams(vmem_limit_bytes=VMEM_LIMIT),
        name="moe_route",
    )(x2, sh, sc, norm_w, rw, rb)

    tm = min(MOE_TILE, n)
    n_tiles = 2 * n // tm + n_e
    rows = (n_tiles + 1) * tm
    e_flat = jnp.concatenate([route[:, 0], route[:, 1]]).astype(jnp.int32)
    onehot = (e_flat[:, None] == jnp.arange(n_e, dtype=jnp.int32)[None, :]).astype(jnp.int32)
    csum = jnp.cumsum(onehot, axis=0)
    counts = csum[-1]
    padded = ((counts + tm - 1) // tm) * tm
    gstart = jnp.cumsum(padded) - padded
    pos = jnp.sum(onehot * (gstart[None, :] + csum - 1), axis=1).astype(jnp.int32)
    pad_start = (gstart + counts).astype(jnp.int32)
    tile_end = jnp.cumsum(padded // tm)
    n_used = tile_end[-1:].astype(jnp.int32)
    tile_ids = jnp.minimum(jnp.arange(n_tiles, dtype=jnp.int32), n_used[0] - 1)
    tile_expert = jnp.sum((tile_ids[:, None] >= tile_end[None, :]).astype(jnp.int32), axis=1)

    chunk = min(GATHER_CHUNK, n)
    per_slot = n // chunk
    xs = pl.pallas_call(
        _gather_kernel,
        grid_spec=pltpu.PrefetchScalarGridSpec(
            num_scalar_prefetch=2,
            grid=(per_slot,),
            in_specs=[pl.BlockSpec((1, 1, chunk), lambda i, pad, nu: (i, 0, 0),
                                   memory_space=pltpu.SMEM),
                      pl.BlockSpec((1, 1, chunk), lambda i, pad, nu: (i + per_slot, 0, 0),
                                   memory_space=pltpu.SMEM),
                      pl.BlockSpec((chunk, 1, d), lambda i, pad, nu: (i, 0, 0))],
            out_specs=pl.BlockSpec(memory_space=pl.ANY),
            scratch_shapes=[pltpu.VMEM((tm, 1, d), F32), pltpu.SemaphoreType.DMA(()),
                            pltpu.SemaphoreType.DMA(())]),
        out_shape=jax.ShapeDtypeStruct((rows, 1, d), F32),
        compiler_params=pltpu.CompilerParams(
            dimension_semantics=("arbitrary",), vmem_limit_bytes=VMEM_LIMIT),
        name="moe_gather",
    )(pad_start, n_used, pos.reshape(2 * per_slot, 1, chunk), pos.reshape(2 * per_slot, 1, chunk), h3)

    tf = min(FFN_TILE_F, d_ff)
    n_f = d_ff // tf

    def row_map(i, f, te, nu):
        return (jnp.minimum(i, nu[0] - 1), 0, 0)

    def f_idx(i, f, nu):
        return jnp.where(i < nu[0], f, n_f - 1)

    ys = pl.pallas_call(
        _moe_ffn_kernel,
        grid_spec=pltpu.PrefetchScalarGridSpec(
            num_scalar_prefetch=2,
            grid=(n_tiles, n_f),
            in_specs=[pl.BlockSpec((tm, 1, d), row_map),
                      pl.BlockSpec((1, d, tf), lambda i, f, te, nu: (te[i], 0, f_idx(i, f, nu))),
                      pl.BlockSpec((1, d, tf), lambda i, f, te, nu: (te[i], 0, f_idx(i, f, nu))),
                      pl.BlockSpec((1, tf, d), lambda i, f, te, nu: (te[i], f_idx(i, f, nu), 0))],
            out_specs=pl.BlockSpec((tm, 1, d), lambda i, f, te, nu: (i, 0, 0)),
            scratch_shapes=[pltpu.VMEM((tm, d), BF16), pltpu.VMEM((tm, d), F32)]),
        out_shape=jax.ShapeDtypeStruct((n_tiles * tm, 1, d), F32),
        compiler_params=pltpu.CompilerParams(
            dimension_semantics=("arbitrary", "arbitrary"), vmem_limit_bytes=VMEM_LIMIT),
        name="moe_ffn",
    )(tile_expert, n_used, xs, w1, w3, w2)

    tc = min(COMBINE_TILE, slen)
    per_seq_c = slen // tc
    n_c = n // tc
    pos3 = pos.reshape(2 * n_c, 1, tc)
    out = pl.pallas_call(
        functools.partial(_combine_kernel, final_norm=final_norm),
        grid=(n_c,),
        in_specs=[pl.BlockSpec((1, 1, tc), lambda i: (i, 0, 0), memory_space=pltpu.SMEM),
                  pl.BlockSpec((1, 1, tc), lambda i: (i + n_c, 0, 0), memory_space=pltpu.SMEM),
                  pl.BlockSpec((tc, d), lambda i: (i, 0)),
                  pl.BlockSpec((tc, ROUTER_LANES), lambda i: (i, 0)),
                  pl.BlockSpec((1, 1, d), lambda i: (i // per_seq_c, 0, 0)),
                  pl.BlockSpec((1, d), lambda i: (0, 0)),
                  pl.BlockSpec(memory_space=pl.ANY)],
        out_specs=pl.BlockSpec((tc, d), lambda i: (i, 0)),
        out_shape=jax.ShapeDtypeStruct((n, d), F32),
        scratch_shapes=[pltpu.VMEM((2, tc, 1, d), F32), pltpu.SemaphoreType.DMA(())],
        compiler_params=pltpu.CompilerParams(
            dimension_semantics=("arbitrary",), vmem_limit_bytes=VMEM_LIMIT),
        name="moe_combine",
    )(pos3, pos3, x2, route, gate, final_w, ys)
    return out.reshape(bsz, slen, d)


def _mix_params(l, norm_mix, w_in, w_out, pool_w, pool_scale, pool_out_norm, sgu_ws, sgu_b,
                sgu_out_norm, conv_w, conv_b, lru_wa, lru_ba, lru_wx, lru_bx, lru_lambda,
                lru_out_norm, lower_bounds, hgrn_out_norm):
    tri = jnp.tril(jnp.ones((SGU_BLOCK, SGU_BLOCK), sgu_ws.dtype))
    sgu_w = jnp.concatenate([sgu_ws[l, hh] * tri for hh in range(sgu_ws.shape[1])], axis=1)
    return {
        "norm": _row(norm_mix[l]),
        "w_in": w_in[l].astype(BF16),
        "w_out": w_out[l].astype(BF16),
        "pool_w": _block_diag(pool_w[l]).astype(BF16),
        "pool_scale": _row(pool_scale[l]),
        "pool_norm": _row(pool_out_norm[l]),
        "sgu_w": sgu_w.astype(BF16),
        "sgu_b": jnp.repeat(sgu_b[l].T, HEAD_DIM, axis=1).astype(F32),
        "sgu_norm": _row(sgu_out_norm[l]),
        "conv_w": conv_w[l].astype(F32),
        "conv_b": _row(conv_b[l]),
        "lru_w": jnp.concatenate([_block_diag(lru_wa[l]), _block_diag(lru_wx[l])], axis=1).astype(BF16),
        "lru_b": _row(jnp.concatenate([lru_ba[l], lru_bx[l]])),
        "lru_log": _row(LRU_C * jax.nn.log_sigmoid(lru_lambda[l].astype(F32))),
        "lru_norm": _row(lru_out_norm[l]),
        "hgrn_lb": _row(lower_bounds[l]),
        "hgrn_norm": _row(hgrn_out_norm[l]),
    }


def kernel(x, c, w_mod, b_mod, norm_mix, norm_ffn, norm_final, w_in, w_out, pool_w, pool_scale, pool_out_norm, sgu_ws, sgu_b, sgu_out_norm, conv_w, conv_b, lru_wa, lru_ba, lru_wx, lru_bx, lru_lambda, lru_out_norm, hgrn_gamma, hgrn_out_norm, ffn_w1, ffn_w3, ffn_w2, router_w, router_b, moe_w1, moe_w3, moe_w2):
    depth = w_mod.shape[0]
    bsz, slen, d = x.shape
    p_layers = jax.nn.softmax(hgrn_gamma.astype(F32), axis=0)
    lower_bounds = jnp.cumsum(p_layers, axis=0) - p_layers[0:1]
    mod = _modulation(c, w_mod, b_mod)
    final_w = _row(norm_final)
    for l in range(depth):
        m = mod[l].reshape(bsz, 6, 1, d)
        sh1, sc1, g1, sh2, sc2, g2 = (m[:, i] for i in range(6))
        p = _mix_params(l, norm_mix, w_in, w_out, pool_w, pool_scale, pool_out_norm, sgu_ws,
                        sgu_b, sgu_out_norm, conv_w, conv_b, lru_wa, lru_ba, lru_wx, lru_bx,
                        lru_lambda, lru_out_norm, lower_bounds, hgrn_out_norm)
        x = _token_mix(x, sh1, sc1, g1, p)
        i = l // 2
        final_norm = l == depth - 1
        if l % 2 == 0:
            x = _ffn_mix(x, sh2, sc2, g2, _row(norm_ffn[l]), ffn_w1[i].astype(BF16),
                         ffn_w3[i].astype(BF16), ffn_w2[i].astype(BF16), final_w,
                         final_norm=final_norm)
        else:
            n_e = router_w.shape[2]
            rw = jnp.pad(router_w[i].astype(F32), ((0, 0), (0, ROUTER_LANES - n_e)))
            rb = jnp.pad(router_b[i].astype(F32), (0, ROUTER_LANES - n_e),
                         constant_values=NEG_BIG).reshape(1, ROUTER_LANES)
            x = _moe_mix(x, sh2, sc2, g2, _row(norm_ffn[l]), rw, rb, moe_w1[i].astype(BF16),
                         moe_w3[i].astype(BF16), moe_w2[i].astype(BF16), final_w,
                         final_norm=final_norm)
    return x
```

```python
import functools

import jax
import jax.numpy as jnp
from jax import lax
from jax.experimental import pallas as pl
from jax.experimental.pallas import tpu as pltpu

D_MODEL = 1024
GROUP = 256
N_IN_SLICES = 9
POOL_WINDOWS = (2, 4, 8, 16)
POOL_HIST = 16
CONV_WIDTH = 4
CONV_HIST = 8
SGU_BLOCK = 128
HEAD_DIM = 64
CHUNK = 64
LRU_C = 8.0
D_FF = 3584
N_EXPERTS = 8
EPS = 1e-6
MIN_FORGET = 1e-30
ROUTER_LANES = 128
NEG_BIG = -1e30
SUBLANES = 8

MIX_TILE = 512
FFN_TILE_M = 1024
FFN_TILE_F = 512
MOE_TILE = 1024
GATHER_CHUNK = 1024
COMBINE_TILE = 512
VMEM_LIMIT = 56 * 1024 * 1024

BF16 = jnp.bfloat16
F32 = jnp.float32


def _dot(a, b):
    return jnp.dot(a, b, preferred_element_type=F32)


def _dot_nt(a, b):
    return lax.dot_general(a, b, (((1,), (1,)), ((), ())), preferred_element_type=F32)


def _dot_tn(a, b):
    return lax.dot_general(a, b, (((0,), (0,)), ((), ())), preferred_element_type=F32)


def _gelu(x):
    return 0.5 * x * (1.0 + jnp.tanh(0.7978845608028654 * (x + 0.044715 * (x * x * x))))


def _logistic(x):
    return 1.0 / (1.0 + jnp.exp(-x))


def _sigmoid(x):
    return 0.5 * jnp.tanh(0.5 * x) + 0.5


def _silu(x):
    return x * _sigmoid(x)


def _rms(x, g):
    return x * lax.rsqrt(jnp.mean(x * x, axis=-1, keepdims=True) + EPS) * g


def _split_dot(x, m):
    hi = x.astype(BF16)
    lo = (x - hi.astype(F32)).astype(BF16)
    return _dot(hi, m) + _dot(lo, m)


def _mod_kernel(c_ref, w_ref, b_ref, o_ref):
    o_ref[0] = _dot(_silu(c_ref[...]), w_ref[0]) + b_ref[0]


def _modulation(c, w_mod, b_mod):
    depth, d, n = w_mod.shape
    bsz = c.shape[0]
    tn = 1536
    return pl.pallas_call(
        _mod_kernel,
        grid=(depth, n // tn),
        in_specs=[
            pl.BlockSpec((bsz, d), lambda l, j: (0, 0)),
            pl.BlockSpec((1, d, tn), lambda l, j: (l, 0, j)),
            pl.BlockSpec((1, 1, tn), lambda l, j: (l, 0, j)),
        ],
        out_specs=pl.BlockSpec((1, bsz, tn), lambda l, j: (l, 0, j)),
        out_shape=jax.ShapeDtypeStruct((depth, bsz, n), F32),
        compiler_params=pltpu.CompilerParams(vmem_limit_bytes=VMEM_LIMIT),
        name="adaln_modulation",
    )(c, w_mod, b_mod.reshape(depth, 1, n))


def _mix_kernel(x_ref, sh_ref, sc_ref, gate_ref, nw_ref, win_ref, wout_ref,
                poolw_ref, poolscale_ref, poolnorm_ref,
                sguw_ref, sgub_ref, sgunorm_ref,
                convw_ref, convb_ref, lruw_ref, lrub_ref, lrulog_ref, lrunorm_ref,
                lb_ref, hgrnnorm_ref,
                o_ref,
                pool_hist, conv_hist, lru_carry, hgrn_state, scan_a, scan_b, proj_scr):
    tile = x_ref.shape[1]
    j = pl.program_id(1)

    @pl.when(j == 0)
    def _():
        pool_hist[...] = jnp.zeros_like(pool_hist)
        conv_hist[...] = jnp.zeros_like(conv_hist)
        lru_carry[...] = jnp.zeros_like(lru_carry)
        hgrn_state[...] = jnp.zeros_like(hgrn_state)

    x = x_ref[0]
    h = _rms(x, nw_ref[...]) * (1.0 + sc_ref[0]) + sh_ref[0]
    hb = h.astype(BF16)

    proj_scr[...] = _dot(hb, win_ref[...])

    def proj(i):
        return proj_scr[:, i * GROUP:(i + 1) * GROUP]

    lane = lax.broadcasted_iota(jnp.int32, (1, GROUP), 1)
    lane_head = lane // HEAD_DIM
    row = lax.broadcasted_iota(jnp.int32, (tile, 1), 0)

    bd_r = lax.broadcasted_iota(jnp.int32, (GROUP, GROUP), 0) // HEAD_DIM
    bd_c = lax.broadcasted_iota(jnp.int32, (GROUP, GROUP), 1) // HEAD_DIM
    bd_mask = (bd_r == bd_c).astype(F32)
    bd_ones = bd_mask.astype(BF16)

    pa = proj(0)
    ext = jnp.concatenate([pool_hist[...], pa], axis=0)
    s2 = ext + pltpu.roll(ext, 1, 0)
    s4 = s2 + pltpu.roll(s2, 2, 0)
    s8 = s4 + pltpu.roll(s4, 4, 0)
    s16 = s8 + pltpu.roll(s8, 8, 0)
    pool_hist[...] = pa[tile - POOL_HIST:, :]
    wsum = jnp.where(lane_head == 0, s2, jnp.where(lane_head == 1, s4,
                     jnp.where(lane_head == 2, s8, s16)))[POOL_HIST:, :]
    win = jnp.where(lane_head == 0, 2.0, jnp.where(lane_head == 1, 4.0,
                    jnp.where(lane_head == 2, 8.0, 16.0)))
    pos = (row + j * tile + 1).astype(F32)
    d = wsum / jnp.minimum(pos, win) - pa
    ya = _dot(d.astype(BF16), poolw_ref[...]) * poolscale_ref[...]
    ya = _rms(ya, poolnorm_ref[...])

    u = _gelu(proj(1))
    vf = _gelu(proj(2))
    mu = jnp.mean(vf, axis=-1, keepdims=True)
    vc = vf - mu
    var = jnp.mean(vc * vc, axis=-1, keepdims=True)
    vn = vc * lax.rsqrt(var + EPS)
    zs = []
    for b in range(tile // SGU_BLOCK):
        vb = vn[b * SGU_BLOCK:(b + 1) * SGU_BLOCK, :]
        stack = jnp.concatenate(
            [jnp.where(lane_head == hh, vb, 0.0) for hh in range(GROUP // HEAD_DIM)], axis=0)
        zs.append(_dot(sguw_ref[...], stack.astype(BF16)) + sgub_ref[...])
    z = jnp.concatenate(zs, axis=0)
    yb = _rms(u * z, sgunorm_ref[...])

    px = proj(3)
    ext = jnp.concatenate([conv_hist[...], px], axis=0)
    xc = (convb_ref[...] + convw_ref[3:4, :] * ext + convw_ref[2:3, :] * pltpu.roll(ext, 1, 0)
          + convw_ref[1:2, :] * pltpu.roll(ext, 2, 0) + convw_ref[0:1, :] * pltpu.roll(ext, 3, 0))
    xc = xc[CONV_HIST:, :]
    conv_hist[...] = px[tile - CONV_HIST:, :]
    gates = _dot(xc.astype(BF16), lruw_ref[...]) + lrub_ref[...]
    r = _sigmoid(gates[:, :GROUP])
    ig = _sigmoid(gates[:, GROUP:])
    log_a = r * lrulog_ref[...]
    a = jnp.exp(log_a)
    bb = jnp.sqrt(jnp.maximum(1.0 - jnp.exp(2.0 * log_a), 0.0)) * (ig * xc)
    a3 = a.reshape(tile // SUBLANES, SUBLANES, GROUP)
    b3 = bb.reshape(tile // SUBLANES, SUBLANES, GROUP)
    sub = lax.broadcasted_iota(jnp.int32, (1, SUBLANES, 1), 1)
    for s in (1, 2, 4):
        keep = sub >= s
        b3 = jnp.where(keep, a3 * pltpu.roll(b3, s, 1) + b3, b3)
        a3 = jnp.where(keep, a3 * pltpu.roll(a3, s, 1), a3)
    scan_a[...] = a3.reshape(tile, GROUP)
    scan_b[...] = b3.reshape(tile, GROUP)

    def lru_step(g, carry):
        r0 = pl.multiple_of(g * SUBLANES, SUBLANES)
        h8 = scan_a[pl.ds(r0, SUBLANES), :] * carry + scan_b[pl.ds(r0, SUBLANES), :]
        scan_b[pl.ds(r0, SUBLANES), :] = h8
        return h8[SUBLANES - 1:SUBLANES, :]

    carry = lax.fori_loop(0, tile // SUBLANES, lru_step, lru_carry[0:1, :], unroll=True)
    lru_carry[...] = jnp.broadcast_to(carry, lru_carry.shape)
    yc = _rms(_gelu(proj(4)) * scan_b[...], lrunorm_ref[...])

    lb = lb_ref[...]
    qf = _silu(proj(5))
    sig = _logistic(proj(6))
    log_f = jnp.log(jnp.maximum(lb + (1.0 - lb) * sig, MIN_FORGET))
    kk = (1.0 - lb) * (1.0 - sig)
    vv = proj(7)
    csub = row % CHUNK
    bcum = log_f
    for s in (1, 2, 4, 8, 16, 32):
        bcum = jnp.where(csub >= s, bcum + pltpu.roll(bcum, s, 0), bcum)
    t_idx = lax.broadcasted_iota(jnp.int32, (CHUNK, GROUP), 0)
    s_idx = lax.broadcasted_iota(jnp.int32, (CHUNK, GROUP), 1) % HEAD_DIM
    causal = (t_idx >= s_idx).astype(F32)
    n_heads = GROUP // HEAD_DIM
    outs = []
    state = hgrn_state[...]
    for c in range(tile // CHUNK):
        sl = slice(c * CHUNK, (c + 1) * CHUNK)
        bc = bcum[sl]
        bmid = bc[CHUNK // 2 - 1:CHUNK // 2, :]
        blast = bc[CHUNK - 1:CHUNK, :]
        qm = qf[sl] * jnp.exp(bc - bmid)
        km = kk[sl] * jnp.exp(bmid - bc)
        qe = qm * jnp.exp(bmid)
        kl = km * jnp.exp(blast - bmid)
        vcb = vv[sl].astype(BF16)
        k_bd = (jnp.concatenate([km] * n_heads, axis=0) * bd_mask).astype(BF16)
        scores = _dot_nt(qm.astype(BF16), k_bd) * causal
        v_bd = (jnp.concatenate([vv[sl]] * n_heads, axis=0) * bd_mask).astype(BF16)
        o = _dot(scores.astype(BF16), v_bd) + _dot_nt(qe.astype(BF16), state.astype(BF16))
        state = state * jnp.exp(blast) + _dot_tn(vcb, kl.astype(BF16)) * bd_mask
        outs.append(o)
    hgrn_state[...] = state
    o = jnp.concatenate(outs, axis=0)
    ms = _split_dot(o * o, bd_ones) * (1.0 / HEAD_DIM)
    o = o * lax.rsqrt(ms + EPS) * hgrnnorm_ref[...]
    yd = o * _silu(proj(8))
    ycat = jnp.concatenate([ya.astype(BF16), yb.astype(BF16), yc.astype(BF16), yd.astype(BF16)],
                           axis=1)
    y = _dot(ycat, wout_ref[...])

    o_ref[0] = x + gate_ref[0] * y


def _row(v):
    return v.reshape(1, -1).astype(F32)


def _block_diag(w):
    n, a, b = w.shape
    eye = jnp.eye(n, dtype=w.dtype)
    return (eye[:, None, :, None] * w[:, :, None, :]).reshape(n * a, n * b)


def _token_mix(x, sh, sc, gate, p):
    bsz, slen, d = x.shape
    tile = min(MIX_TILE, slen)
    full = lambda a: pl.BlockSpec(a.shape, lambda b, j: (0,) * a.ndim)
    per_batch = pl.BlockSpec((1, 1, d), lambda b, j: (b, 0, 0))
    x_spec = pl.BlockSpec((1, tile, d), lambda b, j: (b, j, 0))
    params = [p["norm"], p["w_in"], p["w_out"], p["pool_w"], p["pool_scale"], p["pool_norm"],
              p["sgu_w"], p["sgu_b"], p["sgu_norm"], p["conv_w"], p["conv_b"], p["lru_w"],
              p["lru_b"], p["lru_log"], p["lru_norm"], p["hgrn_lb"], p["hgrn_norm"]]
    return pl.pallas_call(
        _mix_kernel,
        grid=(bsz, slen // tile),
        in_specs=[x_spec, per_batch, per_batch, per_batch] + [full(a) for a in params],
        out_specs=x_spec,
        out_shape=jax.ShapeDtypeStruct(x.shape, x.dtype),
        scratch_shapes=[
            pltpu.VMEM((POOL_HIST, GROUP), F32),
            pltpu.VMEM((CONV_HIST, GROUP), F32),
            pltpu.VMEM((SUBLANES, GROUP), F32),
            pltpu.VMEM((GROUP, GROUP), F32),
            pltpu.VMEM((tile, GROUP), F32),
            pltpu.VMEM((tile, GROUP), F32),
            pltpu.VMEM((tile, N_IN_SLICES * GROUP), F32),
        ],
        compiler_params=pltpu.CompilerParams(
            dimension_semantics=("arbitrary", "arbitrary"), vmem_limit_bytes=VMEM_LIMIT),
        name="token_mix",
    )(x, sh, sc, gate, *params)


def _ffn_kernel(x_ref, sh_ref, sc_ref, gate_ref, nw_ref, w1_ref, w3_ref, w2_ref, fw_ref,
                o_ref, h_scr, acc, *, final_norm):
    f = pl.program_id(1)

    @pl.when(f == 0)
    def _():
        h = _rms(x_ref[...], nw_ref[...]) * (1.0 + sc_ref[0]) + sh_ref[0]
        h_scr[...] = h.astype(BF16)
        acc[...] = jnp.zeros_like(acc)

    hb = h_scr[...]
    act = _silu(_dot(hb, w1_ref[...])) * _dot(hb, w3_ref[...])
    acc[...] += _dot(act.astype(BF16), w2_ref[...])

    @pl.when(f == pl.num_programs(1) - 1)
    def _():
        out = x_ref[...] + gate_ref[0] * acc[...]
        if final_norm:
            out = _rms(out, fw_ref[...])
        o_ref[...] = out


def _route_kernel(x_ref, sh_ref, sc_ref, nw_ref, rw_ref, rb_ref, h_ref, route_ref):
    h = _rms(x_ref[...], nw_ref[...]) * (1.0 + sc_ref[0]) + sh_ref[0]
    h_ref[...] = h
    h_hi = h.astype(BF16)
    h_lo = (h - h_hi.astype(F32)).astype(BF16)
    logits = (_dot(h_hi, rw_ref[0]) + _dot(h_lo, rw_ref[0]) + _dot(h_hi, rw_ref[1])) + rb_ref[...]
    lane = lax.broadcasted_iota(jnp.int32, logits.shape, 1)
    v1 = jnp.max(logits, axis=-1, keepdims=True)
    i1 = jnp.min(jnp.where(logits == v1, lane, ROUTER_LANES), axis=-1, keepdims=True)
    rest = jnp.where(lane == i1, NEG_BIG, logits)
    v2 = jnp.max(rest, axis=-1, keepdims=True)
    i2 = jnp.min(jnp.where(rest == v2, lane, ROUTER_LANES), axis=-1, keepdims=True)
    p2 = jnp.exp(v2 - v1)
    w_top = 1.0 / (1.0 + p2)
    route_ref[...] = jnp.where(lane == 0, i1.astype(F32), jnp.where(lane == 1, i2.astype(F32),
                               jnp.where(lane == 2, w_top, jnp.where(lane == 3, p2 * w_top, 0.0))))


def _gather_kernel(pad_ref, nu_ref, p1_ref, p2_ref, h_ref, xs_ref, zeros, sem, zsem):
    i = pl.program_id(0)
    chunk = h_ref.shape[0]
    tile = zeros.shape[0]
    n_tiles = xs_ref.shape[0] // tile

    @pl.when(i == 0)
    def _():
        zeros[...] = jnp.zeros_like(zeros)
        for e in range(pad_ref.shape[0]):
            cp = pltpu.make_async_copy(zeros, xs_ref.at[pl.ds(pad_ref[e], tile)], zsem)
            cp.start()
            cp.wait()

        def tail(t, carry):
            cp = pltpu.make_async_copy(
                zeros, xs_ref.at[pl.ds(pl.multiple_of(t * tile, tile), tile)], zsem)
            cp.start()
            cp.wait()
            return carry

        lax.fori_loop(nu_ref[0], n_tiles, tail, 0)

    def issue(r, carry):
        pltpu.make_async_copy(h_ref.at[pl.ds(r, 1)], xs_ref.at[p1_ref[0, 0, r]], sem).start()
        pltpu.make_async_copy(h_ref.at[pl.ds(r, 1)], xs_ref.at[p2_ref[0, 0, r]], sem).start()
        return carry

    lax.fori_loop(0, chunk, issue, 0, unroll=8)
    for _ in range(2):
        pltpu.make_async_copy(h_ref, xs_ref.at[pl.ds(0, chunk), 0], sem).wait()


def _moe_ffn_kernel(te_ref, nu_ref, xs_ref, w1_ref, w3_ref, w2_ref, ys_ref, xbuf, xsem, h_scr, acc):
    i = pl.program_id(0)
    f = pl.program_id(1)
    tm = xbuf.shape[1]

    def slab(t, slot):
        rows = pl.ds(pl.multiple_of(t * tm, tm), tm)
        return pltpu.make_async_copy(xs_ref.at[rows, 0], xbuf.at[slot], xsem.at[slot])

    @pl.when((i == 0) & (f == 0))
    def _():
        slab(0, 0).start()

    @pl.when(i < nu_ref[0])
    def _():
        @pl.when(f == 0)
        def _():
            slot = i % 2

            @pl.when(i + 1 < nu_ref[0])
            def _():
                slab(i + 1, 1 - slot).start()

            slab(i, slot).wait()
            h_scr[...] = xbuf[slot].astype(BF16)
            acc[...] = jnp.zeros_like(acc)

        hb = h_scr[...]
        act = _silu(_dot(hb, w1_ref[0])) * _dot(hb, w3_ref[0])
        acc[...] += _dot(act.astype(BF16), w2_ref[0])

        @pl.when(f == pl.num_programs(1) - 1)
        def _():
            ys_ref[...] = acc[...].reshape(ys_ref.shape)

    @pl.when((i >= nu_ref[0]) & (f == 0))
    def _():
        ys_ref[...] = jnp.zeros_like(ys_ref)


def _combine_kernel(p1_ref, p2_ref, x_ref, route_ref, gate_ref, fw_ref, ys_ref, o_ref,
                    ybuf, sem, *, final_norm):
    tm = x_ref.shape[0]

    def issue(r, carry):
        pltpu.make_async_copy(ys_ref.at[p1_ref[0, 0, r]], ybuf.at[0, pl.ds(r, 1)], sem).start()
        pltpu.make_async_copy(ys_ref.at[p2_ref[0, 0, r]], ybuf.at[1, pl.ds(r, 1)], sem).start()
        return carry

    lax.fori_loop(0, tm, issue, 0, unroll=8)
    for s in range(2):
        pltpu.make_async_copy(ys_ref.at[pl.ds(0, tm), 0], ybuf.at[s], sem).wait()
    route = route_ref[...]
    y = route[:, 2:3] * ybuf[0] + route[:, 3:4] * ybuf[1]
    out = x_ref[...] + gate_ref[0] * y
    if final_norm:
        out = _rms(out, fw_ref[...])
    o_ref[...] = out


def _ffn_mix(x, sh, sc, gate, norm_w, w1, w3, w2, final_w, *, final_norm):
    bsz, slen, d = x.shape
    tm = min(FFN_TILE_M, slen)
    per_seq = slen // tm
    d_ff = w1.shape[1]
    tf = min(FFN_TILE_F, d_ff)
    x2 = x.reshape(bsz * slen, d)
    x_spec = pl.BlockSpec((tm, d), lambda i, f: (i, 0))
    per_batch = pl.BlockSpec((1, 1, d), lambda i, f: (i // per_seq, 0, 0))
    row_spec = pl.BlockSpec((1, d), lambda i, f: (0, 0))
    out = pl.pallas_call(
        functools.partial(_ffn_kernel, final_norm=final_norm),
        grid=(bsz * per_seq, d_ff // tf),
        in_specs=[x_spec, per_batch, per_batch, per_batch, row_spec,
                  pl.BlockSpec((d, tf), lambda i, f: (0, f)),
                  pl.BlockSpec((d, tf), lambda i, f: (0, f)),
                  pl.BlockSpec((tf, d), lambda i, f: (f, 0)),
                  row_spec],
        out_specs=x_spec,
        out_shape=jax.ShapeDtypeStruct(x2.shape, x2.dtype),
        scratch_shapes=[pltpu.VMEM((tm, d), BF16), pltpu.VMEM((tm, d), F32)],
        compiler_params=pltpu.CompilerParams(
            dimension_semantics=("arbitrary", "arbitrary"), vmem_limit_bytes=VMEM_LIMIT),
        name="ffn_mix",
    )(x2, sh, sc, gate, norm_w, w1, w3, w2, final_w)
    return out.reshape(bsz, slen, d)


def _moe_mix(x, sh, sc, gate, norm_w, rw, rb, w1, w3, w2, final_w, *, final_norm):
    bsz, slen, d = x.shape
    n = bsz * slen
    n_e, _, d_ff = w1.shape
    x2 = x.reshape(n, d)

    tr = min(FFN_TILE_M, slen)
    per_seq = slen // tr
    per_batch = pl.BlockSpec((1, 1, d), lambda i: (i // per_seq, 0, 0))
    h3, route = pl.pallas_call(
        _route_kernel,
        grid=(n // tr,),
        in_specs=[pl.BlockSpec((tr, d), lambda i: (i, 0)), per_batch, per_batch,
                  pl.BlockSpec((1, d), lambda i: (0, 0)),
                  pl.BlockSpec(rw.shape, lambda i: (0, 0, 0)),
                  pl.BlockSpec(rb.shape, lambda i: (0, 0))],
        out_specs=[pl.BlockSpec((tr, d), lambda i: (i, 0)),
                   pl.BlockSpec((tr, ROUTER_LANES), lambda i: (i, 0))],
        out_shape=[jax.ShapeDtypeStruct((n, d), F32),
                   jax.ShapeDtypeStruct((n, ROUTER_LANES), F32)],
        compiler_params=pltpu.CompilerParams(vmem_limit_bytes=VMEM_LIMIT),
        name="moe_route",
    )(x2, sh, sc, norm_w, rw, rb)

    tm = min(MOE_TILE, n)
    n_tiles = 2 * n // tm + n_e
    rows = (n_tiles + 1) * tm
    e_flat = jnp.concatenate([route[:, 0], route[:, 1]]).astype(jnp.int32)
    onehot = (e_flat[:, None] == jnp.arange(n_e, dtype=jnp.int32)[None, :]).astype(jnp.int32)
    csum = jnp.cumsum(onehot, axis=0)
    counts = csum[-1]
    padded = ((counts + tm - 1) // tm) * tm
    gstart = jnp.cumsum(padded) - padded
    pos = jnp.sum(onehot * (gstart[None, :] + csum - 1), axis=1).astype(jnp.int32)
    pad_start = (gstart + counts).astype(jnp.int32)
    tile_end = jnp.cumsum(padded // tm)
    n_used = tile_end[-1:].astype(jnp.int32)
    tile_ids = jnp.minimum(jnp.arange(n_tiles, dtype=jnp.int32), n_used[0] - 1)
    tile_expert = jnp.sum((tile_ids[:, None] >= tile_end[None, :]).astype(jnp.int32), axis=1)

    chunk = min(GATHER_CHUNK, n)
    per_slot = n // chunk
    xs = pl.pallas_call(
        _gather_kernel,
        grid_spec=pltpu.PrefetchScalarGridSpec(
            num_scalar_prefetch=2,
            grid=(per_slot,),
            in_specs=[pl.BlockSpec((1, 1, chunk), lambda i, pad, nu: (i, 0, 0),
                                   memory_space=pltpu.SMEM),
                      pl.BlockSpec((1, 1, chunk), lambda i, pad, nu: (i + per_slot, 0, 0),
                                   memory_space=pltpu.SMEM),
                      pl.BlockSpec((chunk, d), lambda i, pad, nu: (i, 0))],
            out_specs=pl.BlockSpec(memory_space=pl.ANY),
            scratch_shapes=[pltpu.VMEM((tm, 1, d), F32), pltpu.SemaphoreType.DMA(()),
                            pltpu.SemaphoreType.DMA(())]),
        out_shape=jax.ShapeDtypeStruct((rows, 1, d), F32),
        compiler_params=pltpu.CompilerParams(
            dimension_semantics=("arbitrary",), vmem_limit_bytes=VMEM_LIMIT),
        name="moe_gather",
    )(pad_start, n_used, pos.reshape(2 * per_slot, 1, chunk), pos.reshape(2 * per_slot, 1, chunk), h3)

    tf = min(FFN_TILE_F, d_ff)
    n_f = d_ff // tf

    def f_idx(i, f, nu):
        return jnp.where(i < nu[0], f, n_f - 1)

    ys = pl.pallas_call(
        _moe_ffn_kernel,
        grid_spec=pltpu.PrefetchScalarGridSpec(
            num_scalar_prefetch=2,
            grid=(n_tiles, n_f),
            in_specs=[pl.BlockSpec(memory_space=pl.ANY),
                      pl.BlockSpec((1, d, tf), lambda i, f, te, nu: (te[i], 0, f_idx(i, f, nu))),
                      pl.BlockSpec((1, d, tf), lambda i, f, te, nu: (te[i], 0, f_idx(i, f, nu))),
                      pl.BlockSpec((1, tf, d), lambda i, f, te, nu: (te[i], f_idx(i, f, nu), 0))],
            out_specs=pl.BlockSpec((tm, 1, d), lambda i, f, te, nu: (i, 0, 0)),
            scratch_shapes=[pltpu.VMEM((2, tm, d), F32), pltpu.SemaphoreType.DMA((2,)),
                            pltpu.VMEM((tm, d), BF16), pltpu.VMEM((tm, d), F32)]),
        out_shape=jax.ShapeDtypeStruct((n_tiles * tm, 1, d), F32),
        compiler_params=pltpu.CompilerParams(
            dimension_semantics=("arbitrary", "arbitrary"), vmem_limit_bytes=VMEM_LIMIT),
        name="moe_ffn",
    )(tile_expert, n_used, xs, w1, w3, w2)

    tc = min(COMBINE_TILE, slen)
    per_seq_c = slen // tc
    n_c = n // tc
    pos3 = pos.reshape(2 * n_c, 1, tc)
    out = pl.pallas_call(
        functools.partial(_combine_kernel, final_norm=final_norm),
        grid=(n_c,),
        in_specs=[pl.BlockSpec((1, 1, tc), lambda i: (i, 0, 0), memory_space=pltpu.SMEM),
                  pl.BlockSpec((1, 1, tc), lambda i: (i + n_c, 0, 0), memory_space=pltpu.SMEM),
                  pl.BlockSpec((tc, d), lambda i: (i, 0)),
                  pl.BlockSpec((tc, ROUTER_LANES), lambda i: (i, 0)),
                  pl.BlockSpec((1, 1, d), lambda i: (i // per_seq_c, 0, 0)),
                  pl.BlockSpec((1, d), lambda i: (0, 0)),
                  pl.BlockSpec(memory_space=pl.ANY)],
        out_specs=pl.BlockSpec((tc, d), lambda i: (i, 0)),
        out_shape=jax.ShapeDtypeStruct((n, d), F32),
        scratch_shapes=[pltpu.VMEM((2, tc, d), F32), pltpu.SemaphoreType.DMA(())],
        compiler_params=pltpu.CompilerParams(
            dimension_semantics=("arbitrary",), vmem_limit_bytes=VMEM_LIMIT),
        name="moe_combine",
    )(pos3, pos3, x2, route, gate, final_w, ys)
    return out.reshape(bsz, slen, d)


def _mix_params(l, norm_mix, w_in, w_out, pool_w, pool_scale, pool_out_norm, sgu_ws, sgu_b,
                sgu_out_norm, conv_w, conv_b, lru_wa, lru_ba, lru_wx, lru_bx, lru_lambda,
                lru_out_norm, lower_bounds, hgrn_out_norm):
    tri = jnp.tril(jnp.ones((SGU_BLOCK, SGU_BLOCK), sgu_ws.dtype))
    sgu_w = jnp.concatenate([sgu_ws[l, hh] * tri for hh in range(sgu_ws.shape[1])], axis=1)
    return {
        "norm": _row(norm_mix[l]),
        "w_in": w_in[l].astype(BF16),
        "w_out": w_out[l].astype(BF16),
        "pool_w": _block_diag(pool_w[l]).astype(BF16),
        "pool_scale": _row(pool_scale[l]),
        "pool_norm": _row(pool_out_norm[l]),
        "sgu_w": sgu_w.astype(BF16),
        "sgu_b": jnp.repeat(sgu_b[l].T, HEAD_DIM, axis=1).astype(F32),
        "sgu_norm": _row(sgu_out_norm[l]),
        "conv_w": conv_w[l].astype(F32),
        "conv_b": _row(conv_b[l]),
        "lru_w": jnp.concatenate([_block_diag(lru_wa[l]), _block_diag(lru_wx[l])], axis=1).astype(BF16),
        "lru_b": _row(jnp.concatenate([lru_ba[l], lru_bx[l]])),
        "lru_log": _row(LRU_C * jax.nn.log_sigmoid(lru_lambda[l].astype(F32))),
        "lru_norm": _row(lru_out_norm[l]),
        "hgrn_lb": _row(lower_bounds[l]),
        "hgrn_norm": _row(hgrn_out_norm[l]),
    }


def kernel(x, c, w_mod, b_mod, norm_mix, norm_ffn, norm_final, w_in, w_out, pool_w, pool_scale, pool_out_norm, sgu_ws, sgu_b, sgu_out_norm, conv_w, conv_b, lru_wa, lru_ba, lru_wx, lru_bx, lru_lambda, lru_out_norm, hgrn_gamma, hgrn_out_norm, ffn_w1, ffn_w3, ffn_w2, router_w, router_b, moe_w1, moe_w3, moe_w2):
    depth = w_mod.shape[0]
    bsz, slen, d = x.shape
    p_layers = jax.nn.softmax(hgrn_gamma.astype(F32), axis=0)
    lower_bounds = jnp.cumsum(p_layers, axis=0) - p_layers[0:1]
    mod = _modulation(c, w_mod, b_mod)
    final_w = _row(norm_final)
    for l in range(depth):
        m = mod[l].reshape(bsz, 6, 1, d)
        sh1, sc1, g1, sh2, sc2, g2 = (m[:, i] for i in range(6))
        p = _mix_params(l, norm_mix, w_in, w_out, pool_w, pool_scale, pool_out_norm, sgu_ws,
                        sgu_b, sgu_out_norm, conv_w, conv_b, lru_wa, lru_ba, lru_wx, lru_bx,
                        lru_lambda, lru_out_norm, lower_bounds, hgrn_out_norm)
        x = _token_mix(x, sh1, sc1, g1, p)
        i = l // 2
        final_norm = l == depth - 1
        if l % 2 == 0:
            x = _ffn_mix(x, sh2, sc2, g2, _row(norm_ffn[l]), ffn_w1[i].astype(BF16),
                         ffn_w3[i].astype(BF16), ffn_w2[i].astype(BF16), final_w,
                         final_norm=final_norm)
        else:
            n_e = router_w.shape[2]
            rw = jnp.pad(router_w[i].astype(F32), ((0, 0), (0, ROUTER_LANES - n_e)))
            rw_hi = rw.astype(BF16)
            rw = jnp.stack([rw_hi, (rw - rw_hi.astype(F32)).astype(BF16)])
            rb = jnp.pad(router_b[i].astype(F32), (0, ROUTER_LANES - n_e),
                         constant_values=NEG_BIG).reshape(1, ROUTER_LANES)
            x = _moe_mix(x, sh2, sc2, g2, _row(norm_ffn[l]), rw, rb, moe_w1[i].astype(BF16),
                         moe_w3[i].astype(BF16), moe_w2[i].astype(BF16), final_w,
                         final_norm=final_norm)
    return x
```

```python
import functools

import jax
import jax.numpy as jnp
from jax import lax
from jax.experimental import pallas as pl
from jax.experimental.pallas import tpu as pltpu

D_MODEL = 1024
GROUP = 256
N_IN_SLICES = 9
POOL_WINDOWS = (2, 4, 8, 16)
POOL_HIST = 16
CONV_WIDTH = 4
CONV_HIST = 8
SGU_BLOCK = 128
HEAD_DIM = 64
CHUNK = 64
LRU_C = 8.0
D_FF = 3584
N_EXPERTS = 8
EPS = 1e-6
MIN_FORGET = 1e-30
HGRN_SAFE_RANGE = 80.0
ROUTER_LANES = 128
NEG_BIG = -1e30
SUBLANES = 8

MIX_TILE = 512
FFN_TILE_M = 1024
FFN_TILE_F = 512
MOE_TILE = 1024
GATHER_CHUNK = 1024
COMBINE_TILE = 512
VMEM_LIMIT = 56 * 1024 * 1024

BF16 = jnp.bfloat16
F32 = jnp.float32


def _dot(a, b):
    return jnp.dot(a, b, preferred_element_type=F32)


def _dot_nt(a, b):
    return lax.dot_general(a, b, (((1,), (1,)), ((), ())), preferred_element_type=F32)


def _dot_tn(a, b):
    return lax.dot_general(a, b, (((0,), (0,)), ((), ())), preferred_element_type=F32)


def _gelu(x):
    return 0.5 * x * (1.0 + jnp.tanh(0.7978845608028654 * (x + 0.044715 * (x * x * x))))


def _logistic(x):
    return 1.0 / (1.0 + jnp.exp(-x))


def _sigmoid(x):
    return 0.5 * jnp.tanh(0.5 * x) + 0.5


def _silu(x):
    return x * _sigmoid(x)


def _rms(x, g):
    return x * lax.rsqrt(jnp.mean(x * x, axis=-1, keepdims=True) + EPS) * g


def _split_dot(x, m):
    hi = x.astype(BF16)
    lo = (x - hi.astype(F32)).astype(BF16)
    return _dot(hi, m) + _dot(lo, m)


def _mod_kernel(c_ref, w_ref, b_ref, o_ref):
    o_ref[0] = _dot(_silu(c_ref[...]), w_ref[0]) + b_ref[0]


def _modulation(c, w_mod, b_mod):
    depth, d, n = w_mod.shape
    bsz = c.shape[0]
    tn = 1536
    return pl.pallas_call(
        _mod_kernel,
        grid=(depth, n // tn),
        in_specs=[
            pl.BlockSpec((bsz, d), lambda l, j: (0, 0)),
            pl.BlockSpec((1, d, tn), lambda l, j: (l, 0, j)),
            pl.BlockSpec((1, 1, tn), lambda l, j: (l, 0, j)),
        ],
        out_specs=pl.BlockSpec((1, bsz, tn), lambda l, j: (l, 0, j)),
        out_shape=jax.ShapeDtypeStruct((depth, bsz, n), F32),
        compiler_params=pltpu.CompilerParams(vmem_limit_bytes=VMEM_LIMIT),
        name="adaln_modulation",
    )(c, w_mod, b_mod.reshape(depth, 1, n))


def _mix_kernel(x_ref, sh_ref, sc_ref, gate_ref, nw_ref, win_ref, wout_ref,
                poolw_ref, poolscale_ref, poolnorm_ref,
                sguw_ref, sgub_ref, sgunorm_ref,
                convw_ref, convb_ref, lruw_ref, lrub_ref, lrulog_ref, lrunorm_ref,
                lb_ref, hgrnnorm_ref,
                o_ref,
                pool_hist, conv_hist, lru_carry, hgrn_state, scan_a, scan_b, proj_scr,
                o_inter_scr, ycat_scr, fb_scr):
    tile = x_ref.shape[1]
    j = pl.program_id(1)

    @pl.when(j == 0)
    def _():
        pool_hist[...] = jnp.zeros_like(pool_hist)
        conv_hist[...] = jnp.zeros_like(conv_hist)
        lru_carry[...] = jnp.zeros_like(lru_carry)
        hgrn_state[...] = jnp.zeros_like(hgrn_state)

    x = x_ref[0]
    h = _rms(x, nw_ref[...]) * (1.0 + sc_ref[0]) + sh_ref[0]
    hb = h.astype(BF16)

    proj_scr[...] = _dot(hb, win_ref[...])

    def proj(i):
        return proj_scr[:, i * GROUP:(i + 1) * GROUP]

    lane = lax.broadcasted_iota(jnp.int32, (1, GROUP), 1)
    lane_head = lane // HEAD_DIM
    row = lax.broadcasted_iota(jnp.int32, (tile, 1), 0)

    bd_r = lax.broadcasted_iota(jnp.int32, (GROUP, GROUP), 0) // HEAD_DIM
    bd_c = lax.broadcasted_iota(jnp.int32, (GROUP, GROUP), 1) // HEAD_DIM
    bd_mask = (bd_r == bd_c).astype(F32)
    bd_ones = bd_mask.astype(BF16)

    pa = proj(0)
    ext = jnp.concatenate([pool_hist[...], pa], axis=0)
    s2 = ext + pltpu.roll(ext, 1, 0)
    s4 = s2 + pltpu.roll(s2, 2, 0)
    s8 = s4 + pltpu.roll(s4, 4, 0)
    s16 = s8 + pltpu.roll(s8, 8, 0)
    pool_hist[...] = pa[tile - POOL_HIST:, :]
    wsum = jnp.where(lane_head == 0, s2, jnp.where(lane_head == 1, s4,
                     jnp.where(lane_head == 2, s8, s16)))[POOL_HIST:, :]
    win = jnp.where(lane_head == 0, 2.0, jnp.where(lane_head == 1, 4.0,
                    jnp.where(lane_head == 2, 8.0, 16.0)))
    pos = (row + j * tile + 1).astype(F32)
    d = wsum / jnp.minimum(pos, win) - pa
    ya = _dot(d.astype(BF16), poolw_ref[...]) * poolscale_ref[...]
    ya = _rms(ya, poolnorm_ref[...])

    u = _gelu(proj(1))
    vf = _gelu(proj(2))
    mu = jnp.mean(vf, axis=-1, keepdims=True)
    vc = vf - mu
    var = jnp.mean(vc * vc, axis=-1, keepdims=True)
    vn = vc * lax.rsqrt(var + EPS)
    zs = []
    for b in range(tile // SGU_BLOCK):
        vb = vn[b * SGU_BLOCK:(b + 1) * SGU_BLOCK, :]
        stack = jnp.concatenate(
            [jnp.where(lane_head == hh, vb, 0.0) for hh in range(GROUP // HEAD_DIM)], axis=0)
        zs.append(_dot(sguw_ref[...], stack.astype(BF16)) + sgub_ref[...])
    z = jnp.concatenate(zs, axis=0)
    yb = _rms(u * z, sgunorm_ref[...])

    px = proj(3)
    ext = jnp.concatenate([conv_hist[...], px], axis=0)
    xc = (convb_ref[...] + convw_ref[3:4, :] * ext + convw_ref[2:3, :] * pltpu.roll(ext, 1, 0)
          + convw_ref[1:2, :] * pltpu.roll(ext, 2, 0) + convw_ref[0:1, :] * pltpu.roll(ext, 3, 0))
    xc = xc[CONV_HIST:, :]
    conv_hist[...] = px[tile - CONV_HIST:, :]
    gates = _dot(xc.astype(BF16), lruw_ref[...]) + lrub_ref[...]
    r = _sigmoid(gates[:, :GROUP])
    ig = _sigmoid(gates[:, GROUP:])
    log_a = r * lrulog_ref[...]
    a = jnp.exp(log_a)
    bb = jnp.sqrt(jnp.maximum(1.0 - jnp.exp(2.0 * log_a), 0.0)) * (ig * xc)
    a3 = a.reshape(tile // SUBLANES, SUBLANES, GROUP)
    b3 = bb.reshape(tile // SUBLANES, SUBLANES, GROUP)
    sub = lax.broadcasted_iota(jnp.int32, (1, SUBLANES, 1), 1)
    for s in (1, 2, 4):
        keep = sub >= s
        b3 = jnp.where(keep, a3 * pltpu.roll(b3, s, 1) + b3, b3)
        a3 = jnp.where(keep, a3 * pltpu.roll(a3, s, 1), a3)
    scan_a[...] = a3.reshape(tile, GROUP)
    scan_b[...] = b3.reshape(tile, GROUP)

    def lru_step(g, carry):
        r0 = pl.multiple_of(g * SUBLANES, SUBLANES)
        h8 = scan_a[pl.ds(r0, SUBLANES), :] * carry + scan_b[pl.ds(r0, SUBLANES), :]
        scan_b[pl.ds(r0, SUBLANES), :] = h8
        return h8[SUBLANES - 1:SUBLANES, :]

    carry = lax.fori_loop(0, tile // SUBLANES, lru_step, lru_carry[0:1, :], unroll=True)
    lru_carry[...] = jnp.broadcast_to(carry, lru_carry.shape)
    yc = _rms(_gelu(proj(4)) * scan_b[...], lrunorm_ref[...])

    lb = lb_ref[...]
    qf = _silu(proj(5))
    sig = _logistic(proj(6))
    log_f = jnp.log(jnp.maximum(lb + (1.0 - lb) * sig, MIN_FORGET))
    kk = (1.0 - lb) * (1.0 - sig)
    vv = proj(7)
    csub = row % CHUNK
    bcum = log_f
    for s in (1, 2, 4, 8, 16, 32):
        bcum = jnp.where(csub >= s, bcum + pltpu.roll(bcum, s, 0), bcum)
    t_idx = lax.broadcasted_iota(jnp.int32, (CHUNK, GROUP), 0)
    s_idx = lax.broadcasted_iota(jnp.int32, (CHUNK, GROUP), 1) % HEAD_DIM
    causal = (t_idx >= s_idx).astype(F32)
    n_heads = GROUP // HEAD_DIM
    inters, intras = [], []
    span = jnp.zeros((1, GROUP), F32)
    state = hgrn_state[...]
    for c in range(tile // CHUNK):
        sl = slice(c * CHUNK, (c + 1) * CHUNK)
        bc = bcum[sl]
        bmid = bc[CHUNK // 2 - 1:CHUNK // 2, :]
        blast = bc[CHUNK - 1:CHUNK, :]
        span = jnp.maximum(span, jnp.maximum(-bmid, bmid - blast))
        qe = qf[sl] * jnp.exp(bc)
        kl = kk[sl] * jnp.exp(blast - bc)
        qm = qf[sl] * jnp.exp(bc - bmid)
        km = kk[sl] * jnp.exp(bmid - bc)
        vcb = vv[sl].astype(BF16)
        k_bd = (jnp.concatenate([km] * n_heads, axis=0) * bd_mask).astype(BF16)
        scores = _dot_nt(qm.astype(BF16), k_bd) * causal
        v_bd = (jnp.concatenate([vv[sl]] * n_heads, axis=0) * bd_mask).astype(BF16)
        intras.append(_dot(scores.astype(BF16), v_bd))
        inters.append(_dot_nt(qe.astype(BF16), state.astype(BF16)))
        state = state * jnp.exp(blast) + _dot_tn(vcb, kl.astype(BF16)) * bd_mask
    hgrn_state[...] = state
    o_inter = jnp.concatenate(inters, axis=0)
    o_inter_scr[...] = o_inter
    out_gate = _silu(proj(8))

    def hgrn_out(o):
        ms = _split_dot(o * o, bd_ones) * (1.0 / HEAD_DIM)
        return o * lax.rsqrt(ms + EPS) * hgrnnorm_ref[...] * out_gate

    yd = hgrn_out(o_inter + jnp.concatenate(intras, axis=0))
    ycat_scr[...] = jnp.concatenate(
        [ya.astype(BF16), yb.astype(BF16), yc.astype(BF16), yd.astype(BF16)], axis=1)
    o_ref[0] = x + gate_ref[0] * _dot(ycat_scr[...], wout_ref[...])

    @pl.when(jnp.max(span) > HGRN_SAFE_RANGE)
    def _():
        fb_scr[0] = qf
        fb_scr[1] = kk
        fb_scr[2] = vv
        fb_scr[3] = bcum
        t_row = lax.broadcasted_iota(jnp.int32, (CHUNK, 1), 0)

        def chunk_body(c, carry):
            r0 = pl.multiple_of(c * CHUNK, CHUNK)
            q_c = fb_scr[0, pl.ds(r0, CHUNK), :]
            b_c = fb_scr[3, pl.ds(r0, CHUNK), :]

            def pair_body(s, acc):
                k_s = fb_scr[1, pl.ds(r0 + s, 1), :]
                v_s = fb_scr[2, pl.ds(r0 + s, 1), :]
                b_s = fb_scr[3, pl.ds(r0 + s, 1), :]
                w = jnp.where(t_row >= s, jnp.exp(jnp.minimum(b_c - b_s, 0.0)), 0.0)
                return acc + _split_dot(q_c * w * k_s, bd_ones) * v_s

            o_intra = lax.fori_loop(0, CHUNK, pair_body, jnp.zeros((CHUNK, GROUP), F32))
            o_inter_scr[pl.ds(r0, CHUNK), :] = o_inter_scr[pl.ds(r0, CHUNK), :] + o_intra
            return carry

        lax.fori_loop(0, tile // CHUNK, chunk_body, 0)
        ycat_scr[:, 3 * GROUP:] = hgrn_out(o_inter_scr[...]).astype(BF16)
        o_ref[0] = x + gate_ref[0] * _dot(ycat_scr[...], wout_ref[...])


def _row(v):
    return v.reshape(1, -1).astype(F32)


def _block_diag(w):
    n, a, b = w.shape
    eye = jnp.eye(n, dtype=w.dtype)
    return (eye[:, None, :, None] * w[:, :, None, :]).reshape(n * a, n * b)


def _token_mix(x, sh, sc, gate, p):
    bsz, slen, d = x.shape
    tile = min(MIX_TILE, slen)
    full = lambda a: pl.BlockSpec(a.shape, lambda b, j: (0,) * a.ndim)
    per_batch = pl.BlockSpec((1, 1, d), lambda b, j: (b, 0, 0))
    x_spec = pl.BlockSpec((1, tile, d), lambda b, j: (b, j, 0))
    params = [p["norm"], p["w_in"], p["w_out"], p["pool_w"], p["pool_scale"], p["pool_norm"],
              p["sgu_w"], p["sgu_b"], p["sgu_norm"], p["conv_w"], p["conv_b"], p["lru_w"],
              p["lru_b"], p["lru_log"], p["lru_norm"], p["hgrn_lb"], p["hgrn_norm"]]
    return pl.pallas_call(
        _mix_kernel,
        grid=(bsz, slen // tile),
        in_specs=[x_spec, per_batch, per_batch, per_batch] + [full(a) for a in params],
        out_specs=x_spec,
        out_shape=jax.ShapeDtypeStruct(x.shape, x.dtype),
        scratch_shapes=[
            pltpu.VMEM((POOL_HIST, GROUP), F32),
            pltpu.VMEM((CONV_HIST, GROUP), F32),
            pltpu.VMEM((SUBLANES, GROUP), F32),
            pltpu.VMEM((GROUP, GROUP), F32),
            pltpu.VMEM((tile, GROUP), F32),
            pltpu.VMEM((tile, GROUP), F32),
            pltpu.VMEM((tile, N_IN_SLICES * GROUP), F32),
            pltpu.VMEM((tile, GROUP), F32),
            pltpu.VMEM((tile, D_MODEL), BF16),
            pltpu.VMEM((4, tile, GROUP), F32),
        ],
        compiler_params=pltpu.CompilerParams(
            dimension_semantics=("arbitrary", "arbitrary"), vmem_limit_bytes=VMEM_LIMIT),
        name="token_mix",
    )(x, sh, sc, gate, *params)


def _ffn_kernel(x_ref, sh_ref, sc_ref, gate_ref, nw_ref, w1_ref, w3_ref, w2_ref, fw_ref,
                o_ref, h_scr, acc, *, final_norm):
    f = pl.program_id(1)

    @pl.when(f == 0)
    def _():
        h = _rms(x_ref[...], nw_ref[...]) * (1.0 + sc_ref[0]) + sh_ref[0]
        h_scr[...] = h.astype(BF16)
        acc[...] = jnp.zeros_like(acc)

    hb = h_scr[...]
    act = _silu(_dot(hb, w1_ref[...])) * _dot(hb, w3_ref[...])
    acc[...] += _dot(act.astype(BF16), w2_ref[...])

    @pl.when(f == pl.num_programs(1) - 1)
    def _():
        out = x_ref[...] + gate_ref[0] * acc[...]
        if final_norm:
            out = _rms(out, fw_ref[...])
        o_ref[...] = out


def _route_kernel(x_ref, sh_ref, sc_ref, nw_ref, rw_ref, rb_ref, h_ref, route_ref):
    h = _rms(x_ref[...], nw_ref[...]) * (1.0 + sc_ref[0]) + sh_ref[0]
    h_ref[...] = h
    h_hi = h.astype(BF16)
    h_lo = (h - h_hi.astype(F32)).astype(BF16)
    logits = (_dot(h_hi, rw_ref[0]) + _dot(h_lo, rw_ref[0]) + _dot(h_hi, rw_ref[1])) + rb_ref[...]
    lane = lax.broadcasted_iota(jnp.int32, logits.shape, 1)
    v1 = jnp.max(logits, axis=-1, keepdims=True)
    i1 = jnp.min(jnp.where(logits == v1, lane, ROUTER_LANES), axis=-1, keepdims=True)
    rest = jnp.where(lane == i1, NEG_BIG, logits)
    v2 = jnp.max(rest, axis=-1, keepdims=True)
    i2 = jnp.min(jnp.where(rest == v2, lane, ROUTER_LANES), axis=-1, keepdims=True)
    p2 = jnp.exp(v2 - v1)
    w_top = 1.0 / (1.0 + p2)
    route_ref[...] = jnp.where(lane == 0, i1.astype(F32), jnp.where(lane == 1, i2.astype(F32),
                               jnp.where(lane == 2, w_top, jnp.where(lane == 3, p2 * w_top, 0.0))))


def _gather_kernel(pad_ref, nu_ref, p1_ref, p2_ref, h_ref, xs_ref, zeros, sem, zsem):
    i = pl.program_id(0)
    chunk = h_ref.shape[0]
    tile = zeros.shape[0]
    n_tiles = xs_ref.shape[0] // tile

    @pl.when(i == 0)
    def _():
        zeros[...] = jnp.zeros_like(zeros)
        for e in range(pad_ref.shape[0]):
            cp = pltpu.make_async_copy(zeros, xs_ref.at[pl.ds(pad_ref[e], tile)], zsem)
            cp.start()
            cp.wait()

        def tail(t, carry):
            cp = pltpu.make_async_copy(
                zeros, xs_ref.at[pl.ds(pl.multiple_of(t * tile, tile), tile)], zsem)
            cp.start()
            cp.wait()
            return carry

        lax.fori_loop(nu_ref[0], n_tiles, tail, 0)

    def issue(r, carry):
        pltpu.make_async_copy(h_ref.at[pl.ds(r, 1)], xs_ref.at[p1_ref[0, 0, r]], sem).start()
        pltpu.make_async_copy(h_ref.at[pl.ds(r, 1)], xs_ref.at[p2_ref[0, 0, r]], sem).start()
        return carry

    lax.fori_loop(0, chunk, issue, 0, unroll=8)
    for _ in range(2):
        pltpu.make_async_copy(h_ref, xs_ref.at[pl.ds(0, chunk), 0], sem).wait()


def _moe_ffn_kernel(te_ref, nu_ref, xs_ref, w1_ref, w3_ref, w2_ref, ys_ref, xbuf, xsem, h_scr, acc):
    i = pl.program_id(0)
    f = pl.program_id(1)
    tm = xbuf.shape[1]

    def slab(t, slot):
        rows = pl.ds(pl.multiple_of(t * tm, tm), tm)
        return pltpu.make_async_copy(xs_ref.at[rows, 0], xbuf.at[slot], xsem.at[slot])

    @pl.when((i == 0) & (f == 0))
    def _():
        slab(0, 0).start()

    @pl.when(i < nu_ref[0])
    def _():
        @pl.when(f == 0)
        def _():
            slot = i % 2

            @pl.when(i + 1 < nu_ref[0])
            def _():
                slab(i + 1, 1 - slot).start()

            slab(i, slot).wait()
            h_scr[...] = xbuf[slot].astype(BF16)
            acc[...] = jnp.zeros_like(acc)

        hb = h_scr[...]
        act = _silu(_dot(hb, w1_ref[0])) * _dot(hb, w3_ref[0])
        acc[...] += _dot(act.astype(BF16), w2_ref[0])

        @pl.when(f == pl.num_programs(1) - 1)
        def _():
            ys_ref[...] = acc[...].reshape(ys_ref.shape)

    @pl.when((i >= nu_ref[0]) & (f == 0))
    def _():
        ys_ref[...] = jnp.zeros_like(ys_ref)


def _combine_kernel(p1_ref, p2_ref, x_ref, route_ref, gate_ref, fw_ref, ys_ref, o_ref,
                    ybuf, sem, *, final_norm):
    tm = x_ref.shape[0]

    def issue(r, carry):
        pltpu.make_async_copy(ys_ref.at[p1_ref[0, 0, r]], ybuf.at[0, pl.ds(r, 1)], sem).start()
        pltpu.make_async_copy(ys_ref.at[p2_ref[0, 0, r]], ybuf.at[1, pl.ds(r, 1)], sem).start()
        return carry

    lax.fori_loop(0, tm, issue, 0, unroll=8)
    for s in range(2):
        pltpu.make_async_copy(ys_ref.at[pl.ds(0, tm), 0], ybuf.at[s], sem).wait()
    route = route_ref[...]
    y = route[:, 2:3] * ybuf[0] + route[:, 3:4] * ybuf[1]
    out = x_ref[...] + gate_ref[0] * y
    if final_norm:
        out = _rms(out, fw_ref[...])
    o_ref[...] = out


def _ffn_mix(x, sh, sc, gate, norm_w, w1, w3, w2, final_w, *, final_norm):
    bsz, slen, d = x.shape
    tm = min(FFN_TILE_M, slen)
    per_seq = slen // tm
    d_ff = w1.shape[1]
    tf = min(FFN_TILE_F, d_ff)
    x2 = x.reshape(bsz * slen, d)
    x_spec = pl.BlockSpec((tm, d), lambda i, f: (i, 0))
    per_batch = pl.BlockSpec((1, 1, d), lambda i, f: (i // per_seq, 0, 0))
    row_spec = pl.BlockSpec((1, d), lambda i, f: (0, 0))
    out = pl.pallas_call(
        functools.partial(_ffn_kernel, final_norm=final_norm),
        grid=(bsz * per_seq, d_ff // tf),
        in_specs=[x_spec, per_batch, per_batch, per_batch, row_spec,
                  pl.BlockSpec((d, tf), lambda i, f: (0, f)),
                  pl.BlockSpec((d, tf), lambda i, f: (0, f)),
                  pl.BlockSpec((tf, d), lambda i, f: (f, 0)),
                  row_spec],
        out_specs=x_spec,
        out_shape=jax.ShapeDtypeStruct(x2.shape, x2.dtype),
        scratch_shapes=[pltpu.VMEM((tm, d), BF16), pltpu.VMEM((tm, d), F32)],
        compiler_params=pltpu.CompilerParams(
            dimension_semantics=("arbitrary", "arbitrary"), vmem_limit_bytes=VMEM_LIMIT),
        name="ffn_mix",
    )(x2, sh, sc, gate, norm_w, w1, w3, w2, final_w)
    return out.reshape(bsz, slen, d)


def _moe_mix(x, sh, sc, gate, norm_w, rw, rb, w1, w3, w2, final_w, *, final_norm):
    bsz, slen, d = x.shape
    n = bsz * slen
    n_e, _, d_ff = w1.shape
    x2 = x.reshape(n, d)

    tr = min(FFN_TILE_M, slen)
    per_seq = slen // tr
    per_batch = pl.BlockSpec((1, 1, d), lambda i: (i // per_seq, 0, 0))
    h3, route = pl.pallas_call(
        _route_kernel,
        grid=(n // tr,),
        in_specs=[pl.BlockSpec((tr, d), lambda i: (i, 0)), per_batch, per_batch,
                  pl.BlockSpec((1, d), lambda i: (0, 0)),
                  pl.BlockSpec(rw.shape, lambda i: (0, 0, 0)),
                  pl.BlockSpec(rb.shape, lambda i: (0, 0))],
        out_specs=[pl.BlockSpec((tr, d), lambda i: (i, 0)),
                   pl.BlockSpec((tr, ROUTER_LANES), lambda i: (i, 0))],
        out_shape=[jax.ShapeDtypeStruct((n, d), F32),
                   jax.ShapeDtypeStruct((n, ROUTER_LANES), F32)],
        compiler_params=pltpu.CompilerParams(vmem_limit_bytes=VMEM_LIMIT),
        name="moe_route",
    )(x2, sh, sc, norm_w, rw, rb)

    tm = min(MOE_TILE, n)
    n_tiles = 2 * n // tm + n_e
    rows = (n_tiles + 1) * tm
    e_flat = jnp.concatenate([route[:, 0], route[:, 1]]).astype(jnp.int32)
    onehot = (e_flat[:, None] == jnp.arange(n_e, dtype=jnp.int32)[None, :]).astype(jnp.int32)
    csum = jnp.cumsum(onehot, axis=0)
    counts = csum[-1]
    padded = ((counts + tm - 1) // tm) * tm
    gstart = jnp.cumsum(padded) - padded
    pos = jnp.sum(onehot * (gstart[None, :] + csum - 1), axis=1).astype(jnp.int32)
    pad_start = (gstart + counts).astype(jnp.int32)
    tile_end = jnp.cumsum(padded // tm)
    n_used = tile_end[-1:].astype(jnp.int32)
    tile_ids = jnp.minimum(jnp.arange(n_tiles, dtype=jnp.int32), n_used[0] - 1)
    tile_expert = jnp.sum((tile_ids[:, None] >= tile_end[None, :]).astype(jnp.int32), axis=1)

    chunk = min(GATHER_CHUNK, n)
    per_slot = n // chunk
    xs = pl.pallas_call(
        _gather_kernel,
        grid_spec=pltpu.PrefetchScalarGridSpec(
            num_scalar_prefetch=2,
            grid=(per_slot,),
            in_specs=[pl.BlockSpec((1, 1, chunk), lambda i, pad, nu: (i, 0, 0),
                                   memory_space=pltpu.SMEM),
                      pl.BlockSpec((1, 1, chunk), lambda i, pad, nu: (i + per_slot, 0, 0),
                                   memory_space=pltpu.SMEM),
                      pl.BlockSpec((chunk, d), lambda i, pad, nu: (i, 0))],
            out_specs=pl.BlockSpec(memory_space=pl.ANY),
            scratch_shapes=[pltpu.VMEM((tm, 1, d), F32), pltpu.SemaphoreType.DMA(()),
                            pltpu.SemaphoreType.DMA(())]),
        out_shape=jax.ShapeDtypeStruct((rows, 1, d), F32),
        compiler_params=pltpu.CompilerParams(
            dimension_semantics=("arbitrary",), vmem_limit_bytes=VMEM_LIMIT),
        name="moe_gather",
    )(pad_start, n_used, pos.reshape(2 * per_slot, 1, chunk), pos.reshape(2 * per_slot, 1, chunk), h3)

    tf = min(FFN_TILE_F, d_ff)
    n_f = d_ff // tf

    def f_idx(i, f, nu):
        return jnp.where(i < nu[0], f, n_f - 1)

    ys = pl.pallas_call(
        _moe_ffn_kernel,
        grid_spec=pltpu.PrefetchScalarGridSpec(
            num_scalar_prefetch=2,
            grid=(n_tiles, n_f),
            in_specs=[pl.BlockSpec(memory_space=pl.ANY),
                      pl.BlockSpec((1, d, tf), lambda i, f, te, nu: (te[i], 0, f_idx(i, f, nu))),
                      pl.BlockSpec((1, d, tf), lambda i, f, te, nu: (te[i], 0, f_idx(i, f, nu))),
                      pl.BlockSpec((1, tf, d), lambda i, f, te, nu: (te[i], f_idx(i, f, nu), 0))],
            out_specs=pl.BlockSpec((tm, 1, d), lambda i, f, te, nu: (i, 0, 0)),
            scratch_shapes=[pltpu.VMEM((2, tm, d), F32), pltpu.SemaphoreType.DMA((2,)),
                            pltpu.VMEM((tm, d), BF16), pltpu.VMEM((tm, d), F32)]),
        out_shape=jax.ShapeDtypeStruct((n_tiles * tm, 1, d), F32),
        compiler_params=pltpu.CompilerParams(
            dimension_semantics=("arbitrary", "arbitrary"), vmem_limit_bytes=VMEM_LIMIT),
        name="moe_ffn",
    )(tile_expert, n_used, xs, w1, w3, w2)

    tc = min(COMBINE_TILE, slen)
    per_seq_c = slen // tc
    n_c = n // tc
    pos3 = pos.reshape(2 * n_c, 1, tc)
    out = pl.pallas_call(
        functools.partial(_combine_kernel, final_norm=final_norm),
        grid=(n_c,),
        in_specs=[pl.BlockSpec((1, 1, tc), lambda i: (i, 0, 0), memory_space=pltpu.SMEM),
                  pl.BlockSpec((1, 1, tc), lambda i: (i + n_c, 0, 0), memory_space=pltpu.SMEM),
                  pl.BlockSpec((tc, d), lambda i: (i, 0)),
                  pl.BlockSpec((tc, ROUTER_LANES), lambda i: (i, 0)),
                  pl.BlockSpec((1, 1, d), lambda i: (i // per_seq_c, 0, 0)),
                  pl.BlockSpec((1, d), lambda i: (0, 0)),
                  pl.BlockSpec(memory_space=pl.ANY)],
        out_specs=pl.BlockSpec((tc, d), lambda i: (i, 0)),
        out_shape=jax.ShapeDtypeStruct((n, d), F32),
        scratch_shapes=[pltpu.VMEM((2, tc, d), F32), pltpu.SemaphoreType.DMA(())],
        compiler_params=pltpu.CompilerParams(
            dimension_semantics=("arbitrary",), vmem_limit_bytes=VMEM_LIMIT),
        name="moe_combine",
    )(pos3, pos3, x2, route, gate, final_w, ys)
    return out.reshape(bsz, slen, d)


def _mix_params(l, norm_mix, w_in, w_out, pool_w, pool_scale, pool_out_norm, sgu_ws, sgu_b,
                sgu_out_norm, conv_w, conv_b, lru_wa, lru_ba, lru_wx, lru_bx, lru_lambda,
                lru_out_norm, lower_bounds, hgrn_out_norm):
    tri = jnp.tril(jnp.ones((SGU_BLOCK, SGU_BLOCK), sgu_ws.dtype))
    sgu_w = jnp.concatenate([sgu_ws[l, hh] * tri for hh in range(sgu_ws.shape[1])], axis=1)
    return {
        "norm": _row(norm_mix[l]),
        "w_in": w_in[l].astype(BF16),
        "w_out": w_out[l].astype(BF16),
        "pool_w": _block_diag(pool_w[l]).astype(BF16),
        "pool_scale": _row(pool_scale[l]),
        "pool_norm": _row(pool_out_norm[l]),
        "sgu_w": sgu_w.astype(BF16),
        "sgu_b": jnp.repeat(sgu_b[l].T, HEAD_DIM, axis=1).astype(F32),
        "sgu_norm": _row(sgu_out_norm[l]),
        "conv_w": conv_w[l].astype(F32),
        "conv_b": _row(conv_b[l]),
        "lru_w": jnp.concatenate([_block_diag(lru_wa[l]), _block_diag(lru_wx[l])], axis=1).astype(BF16),
        "lru_b": _row(jnp.concatenate([lru_ba[l], lru_bx[l]])),
        "lru_log": _row(LRU_C * jax.nn.log_sigmoid(lru_lambda[l].astype(F32))),
        "lru_norm": _row(lru_out_norm[l]),
        "hgrn_lb": _row(lower_bounds[l]),
        "hgrn_norm": _row(hgrn_out_norm[l]),
    }


def kernel(x, c, w_mod, b_mod, norm_mix, norm_ffn, norm_final, w_in, w_out, pool_w, pool_scale, pool_out_norm, sgu_ws, sgu_b, sgu_out_norm, conv_w, conv_b, lru_wa, lru_ba, lru_wx, lru_bx, lru_lambda, lru_out_norm, hgrn_gamma, hgrn_out_norm, ffn_w1, ffn_w3, ffn_w2, router_w, router_b, moe_w1, moe_w3, moe_w2):
    depth = w_mod.shape[0]
    bsz, slen, d = x.shape
    p_layers = jax.nn.softmax(hgrn_gamma.astype(F32), axis=0)
    lower_bounds = jnp.cumsum(p_layers, axis=0) - p_layers[0:1]
    mod = _modulation(c, w_mod, b_mod)
    final_w = _row(norm_final)
    for l in range(depth):
        m = mod[l].reshape(bsz, 6, 1, d)
        sh1, sc1, g1, sh2, sc2, g2 = (m[:, i] for i in range(6))
        p = _mix_params(l, norm_mix, w_in, w_out, pool_w, pool_scale, pool_out_norm, sgu_ws,
                        sgu_b, sgu_out_norm, conv_w, conv_b, lru_wa, lru_ba, lru_wx, lru_bx,
                        lru_lambda, lru_out_norm, lower_bounds, hgrn_out_norm)
        x = _token_mix(x, sh1, sc1, g1, p)
        i = l // 2
        final_norm = l == depth - 1
        if l % 2 == 0:
            x = _ffn_mix(x, sh2, sc2, g2, _row(norm_ffn[l]), ffn_w1[i].astype(BF16),
                         ffn_w3[i].astype(BF16), ffn_w2[i].astype(BF16), final_w,
                         final_norm=final_norm)
        else:
            n_e = router_w.shape[2]
            rw = jnp.pad(router_w[i].astype(F32), ((0, 0), (0, ROUTER_LANES - n_e)))
            rw_hi = rw.astype(BF16)
            rw = jnp.stack([rw_hi, (rw - rw_hi.astype(F32)).astype(BF16)])
            rb = jnp.pad(router_b[i].astype(F32), (0, ROUTER_LANES - n_e),
                         constant_values=NEG_BIG).reshape(1, ROUTER_LANES)
            x = _moe_mix(x, sh2, sc2, g2, _row(norm_ffn[l]), rw, rb, moe_w1[i].astype(BF16),
                         moe_w3[i].astype(BF16), moe_w2[i].astype(BF16), final_w,
                         final_norm=final_norm)
    return x
```

```python
import functools

import jax
import jax.numpy as jnp
from jax import lax
from jax.experimental import pallas as pl
from jax.experimental.pallas import tpu as pltpu

D_MODEL = 1024
GROUP = 256
N_IN_SLICES = 9
POOL_WINDOWS = (2, 4, 8, 16)
POOL_HIST = 16
CONV_WIDTH = 4
CONV_HIST = 8
SGU_BLOCK = 128
HEAD_DIM = 64
CHUNK = 64
LRU_C = 8.0
D_FF = 3584
N_EXPERTS = 8
EPS = 1e-6
MIN_FORGET = 1e-30
HGRN_SAFE_RANGE = 80.0
ROUTER_LANES = 128
NEG_BIG = -1e30
SUBLANES = 8

MIX_TILE = 512
FFN_TILE_M = 1024
FFN_TILE_F = 512
MOE_TILE = 1024
GATHER_CHUNK = 1024
COMBINE_TILE = 512
VMEM_LIMIT = 56 * 1024 * 1024

BF16 = jnp.bfloat16
F32 = jnp.float32


def _dot(a, b):
    return jnp.dot(a, b, preferred_element_type=F32)


def _dot_nt(a, b):
    return lax.dot_general(a, b, (((1,), (1,)), ((), ())), preferred_element_type=F32)


def _dot_tn(a, b):
    return lax.dot_general(a, b, (((0,), (0,)), ((), ())), preferred_element_type=F32)


def _gelu(x):
    return 0.5 * x * (1.0 + jnp.tanh(0.7978845608028654 * (x + 0.044715 * (x * x * x))))


def _logistic(x):
    return 1.0 / (1.0 + jnp.exp(-x))


def _sigmoid(x):
    return 0.5 * jnp.tanh(0.5 * x) + 0.5


def _silu(x):
    return x * _sigmoid(x)


def _rms(x, g):
    return x * lax.rsqrt(jnp.mean(x * x, axis=-1, keepdims=True) + EPS) * g


def _split_dot(x, m):
    hi = x.astype(BF16)
    lo = (x - hi.astype(F32)).astype(BF16)
    return _dot(hi, m) + _dot(lo, m)


def _mod_kernel(c_ref, w_ref, b_ref, o_ref):
    o_ref[0] = _dot(_silu(c_ref[...]), w_ref[0]) + b_ref[0]


def _modulation(c, w_mod, b_mod):
    depth, d, n = w_mod.shape
    bsz = c.shape[0]
    tn = 1536
    return pl.pallas_call(
        _mod_kernel,
        grid=(depth, n // tn),
        in_specs=[
            pl.BlockSpec((bsz, d), lambda l, j: (0, 0)),
            pl.BlockSpec((1, d, tn), lambda l, j: (l, 0, j)),
            pl.BlockSpec((1, 1, tn), lambda l, j: (l, 0, j)),
        ],
        out_specs=pl.BlockSpec((1, bsz, tn), lambda l, j: (l, 0, j)),
        out_shape=jax.ShapeDtypeStruct((depth, bsz, n), F32),
        compiler_params=pltpu.CompilerParams(vmem_limit_bytes=VMEM_LIMIT),
        name="adaln_modulation",
    )(c, w_mod, b_mod.reshape(depth, 1, n))


def _mix_kernel(x_ref, sh_ref, sc_ref, gate_ref, nw_ref, win_ref, wout_ref,
                poolw_ref, poolscale_ref, poolnorm_ref,
                sguw_ref, sgub_ref, sgunorm_ref,
                convw_ref, convb_ref, lruw_ref, lrub_ref, lrulog_ref, lrunorm_ref,
                lb_ref, hgrnnorm_ref,
                o_ref,
                pool_hist, conv_hist, lru_carry, hgrn_state, scan_a, scan_b, proj_scr,
                state0_scr, o_scr, ycat_scr, fb_scr):
    tile = x_ref.shape[1]
    j = pl.program_id(1)

    @pl.when(j == 0)
    def _():
        pool_hist[...] = jnp.zeros_like(pool_hist)
        conv_hist[...] = jnp.zeros_like(conv_hist)
        lru_carry[...] = jnp.zeros_like(lru_carry)
        hgrn_state[...] = jnp.zeros_like(hgrn_state)

    x = x_ref[0]
    h = _rms(x, nw_ref[...]) * (1.0 + sc_ref[0]) + sh_ref[0]
    hb = h.astype(BF16)

    proj_scr[...] = _dot(hb, win_ref[...])

    def proj(i):
        return proj_scr[:, i * GROUP:(i + 1) * GROUP]

    lane = lax.broadcasted_iota(jnp.int32, (1, GROUP), 1)
    lane_head = lane // HEAD_DIM
    row = lax.broadcasted_iota(jnp.int32, (tile, 1), 0)

    bd_r = lax.broadcasted_iota(jnp.int32, (GROUP, GROUP), 0) // HEAD_DIM
    bd_c = lax.broadcasted_iota(jnp.int32, (GROUP, GROUP), 1) // HEAD_DIM
    bd_mask = (bd_r == bd_c).astype(F32)
    bd_ones = bd_mask.astype(BF16)

    pa = proj(0)
    ext = jnp.concatenate([pool_hist[...], pa], axis=0)
    s2 = ext + pltpu.roll(ext, 1, 0)
    s4 = s2 + pltpu.roll(s2, 2, 0)
    s8 = s4 + pltpu.roll(s4, 4, 0)
    s16 = s8 + pltpu.roll(s8, 8, 0)
    pool_hist[...] = pa[tile - POOL_HIST:, :]
    wsum = jnp.where(lane_head == 0, s2, jnp.where(lane_head == 1, s4,
                     jnp.where(lane_head == 2, s8, s16)))[POOL_HIST:, :]
    win = jnp.where(lane_head == 0, 2.0, jnp.where(lane_head == 1, 4.0,
                    jnp.where(lane_head == 2, 8.0, 16.0)))
    pos = (row + j * tile + 1).astype(F32)
    d = wsum / jnp.minimum(pos, win) - pa
    ya = _dot(d.astype(BF16), poolw_ref[...]) * poolscale_ref[...]
    ya = _rms(ya, poolnorm_ref[...])

    u = _gelu(proj(1))
    vf = _gelu(proj(2))
    mu = jnp.mean(vf, axis=-1, keepdims=True)
    vc = vf - mu
    var = jnp.mean(vc * vc, axis=-1, keepdims=True)
    vn = vc * lax.rsqrt(var + EPS)
    zs = []
    for b in range(tile // SGU_BLOCK):
        vb = vn[b * SGU_BLOCK:(b + 1) * SGU_BLOCK, :]
        stack = jnp.concatenate(
            [jnp.where(lane_head == hh, vb, 0.0) for hh in range(GROUP // HEAD_DIM)], axis=0)
        zs.append(_dot(sguw_ref[...], stack.astype(BF16)) + sgub_ref[...])
    z = jnp.concatenate(zs, axis=0)
    yb = _rms(u * z, sgunorm_ref[...])

    px = proj(3)
    ext = jnp.concatenate([conv_hist[...], px], axis=0)
    xc = (convb_ref[...] + convw_ref[3:4, :] * ext + convw_ref[2:3, :] * pltpu.roll(ext, 1, 0)
          + convw_ref[1:2, :] * pltpu.roll(ext, 2, 0) + convw_ref[0:1, :] * pltpu.roll(ext, 3, 0))
    xc = xc[CONV_HIST:, :]
    conv_hist[...] = px[tile - CONV_HIST:, :]
    gates = _dot(xc.astype(BF16), lruw_ref[...]) + lrub_ref[...]
    r = _sigmoid(gates[:, :GROUP])
    ig = _sigmoid(gates[:, GROUP:])
    log_a = r * lrulog_ref[...]
    a = jnp.exp(log_a)
    bb = jnp.sqrt(jnp.maximum(1.0 - jnp.exp(2.0 * log_a), 0.0)) * (ig * xc)
    a3 = a.reshape(tile // SUBLANES, SUBLANES, GROUP)
    b3 = bb.reshape(tile // SUBLANES, SUBLANES, GROUP)
    sub = lax.broadcasted_iota(jnp.int32, (1, SUBLANES, 1), 1)
    for s in (1, 2, 4):
        keep = sub >= s
        b3 = jnp.where(keep, a3 * pltpu.roll(b3, s, 1) + b3, b3)
        a3 = jnp.where(keep, a3 * pltpu.roll(a3, s, 1), a3)
    scan_a[...] = a3.reshape(tile, GROUP)
    scan_b[...] = b3.reshape(tile, GROUP)

    def lru_step(g, carry):
        r0 = pl.multiple_of(g * SUBLANES, SUBLANES)
        h8 = scan_a[pl.ds(r0, SUBLANES), :] * carry + scan_b[pl.ds(r0, SUBLANES), :]
        scan_b[pl.ds(r0, SUBLANES), :] = h8
        return h8[SUBLANES - 1:SUBLANES, :]

    carry = lax.fori_loop(0, tile // SUBLANES, lru_step, lru_carry[0:1, :], unroll=True)
    lru_carry[...] = jnp.broadcast_to(carry, lru_carry.shape)
    yc = _rms(_gelu(proj(4)) * scan_b[...], lrunorm_ref[...])

    lb = lb_ref[...]
    qf = _silu(proj(5))
    sig = _logistic(proj(6))
    log_f = jnp.log(jnp.maximum(lb + (1.0 - lb) * sig, MIN_FORGET))
    kk = (1.0 - lb) * (1.0 - sig)
    vv = proj(7)
    csub = row % CHUNK
    bcum = log_f
    for s in (1, 2, 4, 8, 16, 32):
        bcum = jnp.where(csub >= s, bcum + pltpu.roll(bcum, s, 0), bcum)
    t_idx = lax.broadcasted_iota(jnp.int32, (CHUNK, GROUP), 0)
    s_idx = lax.broadcasted_iota(jnp.int32, (CHUNK, GROUP), 1) % HEAD_DIM
    causal = (t_idx >= s_idx).astype(F32)
    n_heads = GROUP // HEAD_DIM
    outs = []
    span = jnp.zeros((1, GROUP), F32)
    state = hgrn_state[...]
    state0_scr[...] = state
    for c in range(tile // CHUNK):
        sl = slice(c * CHUNK, (c + 1) * CHUNK)
        bc = bcum[sl]
        bmid = bc[CHUNK // 2 - 1:CHUNK // 2, :]
        blast = bc[CHUNK - 1:CHUNK, :]
        span = jnp.maximum(span, jnp.maximum(-bmid, bmid - blast))
        qm = qf[sl] * jnp.exp(bc - bmid)
        km = kk[sl] * jnp.exp(bmid - bc)
        qe = qm * jnp.exp(bmid)
        kl = km * jnp.exp(blast - bmid)
        vcb = vv[sl].astype(BF16)
        k_bd = (jnp.concatenate([km] * n_heads, axis=0) * bd_mask).astype(BF16)
        scores = _dot_nt(qm.astype(BF16), k_bd) * causal
        v_bd = (jnp.concatenate([vv[sl]] * n_heads, axis=0) * bd_mask).astype(BF16)
        outs.append(_dot(scores.astype(BF16), v_bd) + _dot_nt(qe.astype(BF16), state.astype(BF16)))
        state = state * jnp.exp(blast) + _dot_tn(vcb, kl.astype(BF16)) * bd_mask
    out_gate = _silu(proj(8))

    def hgrn_out(o):
        ms = _split_dot(o * o, bd_ones) * (1.0 / HEAD_DIM)
        return o * lax.rsqrt(ms + EPS) * hgrnnorm_ref[...] * out_gate

    yd = hgrn_out(jnp.concatenate(outs, axis=0))
    hgrn_state[...] = state
    ycat_scr[...] = jnp.concatenate(
        [ya.astype(BF16), yb.astype(BF16), yc.astype(BF16), yd.astype(BF16)], axis=1)
    o_ref[0] = x + gate_ref[0] * _dot(ycat_scr[...], wout_ref[...])

    @pl.when(jnp.max(span) > HGRN_SAFE_RANGE)
    def _():
        fb_scr[0] = qf
        fb_scr[1] = kk
        fb_scr[2] = vv
        fb_scr[3] = bcum
        t_row = lax.broadcasted_iota(jnp.int32, (CHUNK, 1), 0)

        def chunk_body(c, st):
            r0 = pl.multiple_of(c * CHUNK, CHUNK)
            q_c = fb_scr[0, pl.ds(r0, CHUNK), :]
            k_c = fb_scr[1, pl.ds(r0, CHUNK), :]
            v_c = fb_scr[2, pl.ds(r0, CHUNK), :]
            b_c = fb_scr[3, pl.ds(r0, CHUNK), :]
            b_end = fb_scr[3, pl.ds(r0 + CHUNK - 1, 1), :]

            def pair_body(s, acc):
                k_s = fb_scr[1, pl.ds(r0 + s, 1), :]
                v_s = fb_scr[2, pl.ds(r0 + s, 1), :]
                b_s = fb_scr[3, pl.ds(r0 + s, 1), :]
                w = jnp.where(t_row >= s, jnp.exp(jnp.minimum(b_c - b_s, 0.0)), 0.0)
                return acc + _split_dot(q_c * w * k_s, bd_ones) * v_s

            o_intra = lax.fori_loop(0, CHUNK, pair_body, jnp.zeros((CHUNK, GROUP), F32))
            o_inter = _dot_nt((q_c * jnp.exp(b_c)).astype(BF16), st.astype(BF16))
            o_scr[pl.ds(r0, CHUNK), :] = o_inter + o_intra
            k_end = k_c * jnp.exp(b_end - b_c)
            return st * jnp.exp(b_end) + _dot_tn(v_c.astype(BF16), k_end.astype(BF16)) * bd_mask

        hgrn_state[...] = lax.fori_loop(0, tile // CHUNK, chunk_body, state0_scr[...])
        ycat_scr[:, 3 * GROUP:] = hgrn_out(o_scr[...]).astype(BF16)
        o_ref[0] = x + gate_ref[0] * _dot(ycat_scr[...], wout_ref[...])


def _row(v):
    return v.reshape(1, -1).astype(F32)


def _block_diag(w):
    n, a, b = w.shape
    eye = jnp.eye(n, dtype=w.dtype)
    return (eye[:, None, :, None] * w[:, :, None, :]).reshape(n * a, n * b)


def _token_mix(x, sh, sc, gate, p):
    bsz, slen, d = x.shape
    tile = min(MIX_TILE, slen)
    full = lambda a: pl.BlockSpec(a.shape, lambda b, j: (0,) * a.ndim)
    per_batch = pl.BlockSpec((1, 1, d), lambda b, j: (b, 0, 0))
    x_spec = pl.BlockSpec((1, tile, d), lambda b, j: (b, j, 0))
    params = [p["norm"], p["w_in"], p["w_out"], p["pool_w"], p["pool_scale"], p["pool_norm"],
              p["sgu_w"], p["sgu_b"], p["sgu_norm"], p["conv_w"], p["conv_b"], p["lru_w"],
              p["lru_b"], p["lru_log"], p["lru_norm"], p["hgrn_lb"], p["hgrn_norm"]]
    return pl.pallas_call(
        _mix_kernel,
        grid=(bsz, slen // tile),
        in_specs=[x_spec, per_batch, per_batch, per_batch] + [full(a) for a in params],
        out_specs=x_spec,
        out_shape=jax.ShapeDtypeStruct(x.shape, x.dtype),
        scratch_shapes=[
            pltpu.VMEM((POOL_HIST, GROUP), F32),
            pltpu.VMEM((CONV_HIST, GROUP), F32),
            pltpu.VMEM((SUBLANES, GROUP), F32),
            pltpu.VMEM((GROUP, GROUP), F32),
            pltpu.VMEM((tile, GROUP), F32),
            pltpu.VMEM((tile, GROUP), F32),
            pltpu.VMEM((tile, N_IN_SLICES * GROUP), F32),
            pltpu.VMEM((GROUP, GROUP), F32),
            pltpu.VMEM((tile, GROUP), F32),
            pltpu.VMEM((tile, D_MODEL), BF16),
            pltpu.VMEM((4, tile, GROUP), F32),
        ],
        compiler_params=pltpu.CompilerParams(
            dimension_semantics=("arbitrary", "arbitrary"), vmem_limit_bytes=VMEM_LIMIT),
        name="token_mix",
    )(x, sh, sc, gate, *params)


def _ffn_kernel(x_ref, sh_ref, sc_ref, gate_ref, nw_ref, w1_ref, w3_ref, w2_ref, fw_ref,
                o_ref, h_scr, acc, *, final_norm):
    f = pl.program_id(1)

    @pl.when(f == 0)
    def _():
        h = _rms(x_ref[...], nw_ref[...]) * (1.0 + sc_ref[0]) + sh_ref[0]
        h_scr[...] = h.astype(BF16)
        acc[...] = jnp.zeros_like(acc)

    hb = h_scr[...]
    act = _silu(_dot(hb, w1_ref[...])) * _dot(hb, w3_ref[...])
    acc[...] += _dot(act.astype(BF16), w2_ref[...])

    @pl.when(f == pl.num_programs(1) - 1)
    def _():
        out = x_ref[...] + gate_ref[0] * acc[...]
        if final_norm:
            out = _rms(out, fw_ref[...])
        o_ref[...] = out


def _route_kernel(x_ref, sh_ref, sc_ref, nw_ref, rw_ref, rb_ref, h_ref, route_ref):
    h = _rms(x_ref[...], nw_ref[...]) * (1.0 + sc_ref[0]) + sh_ref[0]
    h_ref[...] = h
    h_hi = h.astype(BF16)
    h_lo = (h - h_hi.astype(F32)).astype(BF16)
    logits = (_dot(h_hi, rw_ref[0]) + _dot(h_lo, rw_ref[0]) + _dot(h_hi, rw_ref[1])) + rb_ref[...]
    lane = lax.broadcasted_iota(jnp.int32, logits.shape, 1)
    v1 = jnp.max(logits, axis=-1, keepdims=True)
    i1 = jnp.min(jnp.where(logits == v1, lane, ROUTER_LANES), axis=-1, keepdims=True)
    rest = jnp.where(lane == i1, NEG_BIG, logits)
    v2 = jnp.max(rest, axis=-1, keepdims=True)
    i2 = jnp.min(jnp.where(rest == v2, lane, ROUTER_LANES), axis=-1, keepdims=True)
    p2 = jnp.exp(v2 - v1)
    w_top = 1.0 / (1.0 + p2)
    route_ref[...] = jnp.where(lane == 0, i1.astype(F32), jnp.where(lane == 1, i2.astype(F32),
                               jnp.where(lane == 2, w_top, jnp.where(lane == 3, p2 * w_top, 0.0))))


def _gather_kernel(pad_ref, nu_ref, p1_ref, p2_ref, h_ref, xs_ref, zeros, sem, zsem):
    i = pl.program_id(0)
    chunk = h_ref.shape[0]
    tile = zeros.shape[0]
    n_tiles = xs_ref.shape[0] // tile

    @pl.when(i == 0)
    def _():
        zeros[...] = jnp.zeros_like(zeros)
        for e in range(pad_ref.shape[0]):
            cp = pltpu.make_async_copy(zeros, xs_ref.at[pl.ds(pad_ref[e], tile)], zsem)
            cp.start()
            cp.wait()

        def tail(t, carry):
            cp = pltpu.make_async_copy(
                zeros, xs_ref.at[pl.ds(pl.multiple_of(t * tile, tile), tile)], zsem)
            cp.start()
            cp.wait()
            return carry

        lax.fori_loop(nu_ref[0], n_tiles, tail, 0)

    def issue(r, carry):
        pltpu.make_async_copy(h_ref.at[pl.ds(r, 1)], xs_ref.at[p1_ref[0, 0, r]], sem).start()
        pltpu.make_async_copy(h_ref.at[pl.ds(r, 1)], xs_ref.at[p2_ref[0, 0, r]], sem).start()
        return carry

    lax.fori_loop(0, chunk, issue, 0, unroll=8)
    for _ in range(2):
        pltpu.make_async_copy(h_ref, xs_ref.at[pl.ds(0, chunk), 0], sem).wait()


def _moe_ffn_kernel(te_ref, nu_ref, xs_ref, w1_ref, w3_ref, w2_ref, ys_ref, xbuf, xsem, h_scr, acc):
    i = pl.program_id(0)
    f = pl.program_id(1)
    tm = xbuf.shape[1]

    def slab(t, slot):
        rows = pl.ds(pl.multiple_of(t * tm, tm), tm)
        return pltpu.make_async_copy(xs_ref.at[rows, 0], xbuf.at[slot], xsem.at[slot])

    @pl.when((i == 0) & (f == 0))
    def _():
        slab(0, 0).start()

    @pl.when(i < nu_ref[0])
    def _():
        @pl.when(f == 0)
        def _():
            slot = i % 2

            @pl.when(i + 1 < nu_ref[0])
            def _():
                slab(i + 1, 1 - slot).start()

            slab(i, slot).wait()
            h_scr[...] = xbuf[slot].astype(BF16)
            acc[...] = jnp.zeros_like(acc)

        hb = h_scr[...]
        act = _silu(_dot(hb, w1_ref[0])) * _dot(hb, w3_ref[0])
        acc[...] += _dot(act.astype(BF16), w2_ref[0])

        @pl.when(f == pl.num_programs(1) - 1)
        def _():
            ys_ref[...] = acc[...].reshape(ys_ref.shape)

    @pl.when((i >= nu_ref[0]) & (f == 0))
    def _():
        ys_ref[...] = jnp.zeros_like(ys_ref)


def _combine_kernel(p1_ref, p2_ref, x_ref, route_ref, gate_ref, fw_ref, ys_ref, o_ref,
                    ybuf, sem, *, final_norm):
    tm = x_ref.shape[0]

    def issue(r, carry):
        pltpu.make_async_copy(ys_ref.at[p1_ref[0, 0, r]], ybuf.at[0, pl.ds(r, 1)], sem).start()
        pltpu.make_async_copy(ys_ref.at[p2_ref[0, 0, r]], ybuf.at[1, pl.ds(r, 1)], sem).start()
        return carry

    lax.fori_loop(0, tm, issue, 0, unroll=8)
    for s in range(2):
        pltpu.make_async_copy(ys_ref.at[pl.ds(0, tm), 0], ybuf.at[s], sem).wait()
    route = route_ref[...]
    y = route[:, 2:3] * ybuf[0] + route[:, 3:4] * ybuf[1]
    out = x_ref[...] + gate_ref[0] * y
    if final_norm:
        out = _rms(out, fw_ref[...])
    o_ref[...] = out


def _ffn_mix(x, sh, sc, gate, norm_w, w1, w3, w2, final_w, *, final_norm):
    bsz, slen, d = x.shape
    tm = min(FFN_TILE_M, slen)
    per_seq = slen // tm
    d_ff = w1.shape[1]
    tf = min(FFN_TILE_F, d_ff)
    x2 = x.reshape(bsz * slen, d)
    x_spec = pl.BlockSpec((tm, d), lambda i, f: (i, 0))
    per_batch = pl.BlockSpec((1, 1, d), lambda i, f: (i // per_seq, 0, 0))
    row_spec = pl.BlockSpec((1, d), lambda i, f: (0, 0))
    out = pl.pallas_call(
        functools.partial(_ffn_kernel, final_norm=final_norm),
        grid=(bsz * per_seq, d_ff // tf),
        in_specs=[x_spec, per_batch, per_batch, per_batch, row_spec,
                  pl.BlockSpec((d, tf), lambda i, f: (0, f)),
                  pl.BlockSpec((d, tf), lambda i, f: (0, f)),
                  pl.BlockSpec((tf, d), lambda i, f: (f, 0)),
                  row_spec],
        out_specs=x_spec,
        out_shape=jax.ShapeDtypeStruct(x2.shape, x2.dtype),
        scratch_shapes=[pltpu.VMEM((tm, d), BF16), pltpu.VMEM((tm, d), F32)],
        compiler_params=pltpu.CompilerParams(
            dimension_semantics=("arbitrary", "arbitrary"), vmem_limit_bytes=VMEM_LIMIT),
        name="ffn_mix",
    )(x2, sh, sc, gate, norm_w, w1, w3, w2, final_w)
    return out.reshape(bsz, slen, d)


def _moe_mix(x, sh, sc, gate, norm_w, rw, rb, w1, w3, w2, final_w, *, final_norm):
    bsz, slen, d = x.shape
    n = bsz * slen
    n_e, _, d_ff = w1.shape
    x2 = x.reshape(n, d)

    tr = min(FFN_TILE_M, slen)
    per_seq = slen // tr
    per_batch = pl.BlockSpec((1, 1, d), lambda i: (i // per_seq, 0, 0))
    h3, route = pl.pallas_call(
        _route_kernel,
        grid=(n // tr,),
        in_specs=[pl.BlockSpec((tr, d), lambda i: (i, 0)), per_batch, per_batch,
                  pl.BlockSpec((1, d), lambda i: (0, 0)),
                  pl.BlockSpec(rw.shape, lambda i: (0, 0, 0)),
                  pl.BlockSpec(rb.shape, lambda i: (0, 0))],
        out_specs=[pl.BlockSpec((tr, d), lambda i: (i, 0)),
                   pl.BlockSpec((tr, ROUTER_LANES), lambda i: (i, 0))],
        out_shape=[jax.ShapeDtypeStruct((n, d), F32),
                   jax.ShapeDtypeStruct((n, ROUTER_LANES), F32)],
        compiler_params=pltpu.CompilerParams(vmem_limit_bytes=VMEM_LIMIT),
        name="moe_route",
    )(x2, sh, sc, norm_w, rw, rb)

    tm = min(MOE_TILE, n)
    n_tiles = 2 * n // tm + n_e
    rows = (n_tiles + 1) * tm
    e_flat = jnp.concatenate([route[:, 0], route[:, 1]]).astype(jnp.int32)
    onehot = (e_flat[:, None] == jnp.arange(n_e, dtype=jnp.int32)[None, :]).astype(jnp.int32)
    csum = jnp.cumsum(onehot, axis=0)
    counts = csum[-1]
    padded = ((counts + tm - 1) // tm) * tm
    gstart = jnp.cumsum(padded) - padded
    pos = jnp.sum(onehot * (gstart[None, :] + csum - 1), axis=1).astype(jnp.int32)
    pad_start = (gstart + counts).astype(jnp.int32)
    tile_end = jnp.cumsum(padded // tm)
    n_used = tile_end[-1:].astype(jnp.int32)
    tile_ids = jnp.minimum(jnp.arange(n_tiles, dtype=jnp.int32), n_used[0] - 1)
    tile_expert = jnp.sum((tile_ids[:, None] >= tile_end[None, :]).astype(jnp.int32), axis=1)

    chunk = min(GATHER_CHUNK, n)
    per_slot = n // chunk
    xs = pl.pallas_call(
        _gather_kernel,
        grid_spec=pltpu.PrefetchScalarGridSpec(
            num_scalar_prefetch=2,
            grid=(per_slot,),
            in_specs=[pl.BlockSpec((1, 1, chunk), lambda i, pad, nu: (i, 0, 0),
                                   memory_space=pltpu.SMEM),
                      pl.BlockSpec((1, 1, chunk), lambda i, pad, nu: (i + per_slot, 0, 0),
                                   memory_space=pltpu.SMEM),
                      pl.BlockSpec((chunk, d), lambda i, pad, nu: (i, 0))],
            out_specs=pl.BlockSpec(memory_space=pl.ANY),
            scratch_shapes=[pltpu.VMEM((tm, 1, d), F32), pltpu.SemaphoreType.DMA(()),
                            pltpu.SemaphoreType.DMA(())]),
        out_shape=jax.ShapeDtypeStruct((rows, 1, d), F32),
        compiler_params=pltpu.CompilerParams(
            dimension_semantics=("arbitrary",), vmem_limit_bytes=VMEM_LIMIT),
        name="moe_gather",
    )(pad_start, n_used, pos.reshape(2 * per_slot, 1, chunk), pos.reshape(2 * per_slot, 1, chunk), h3)

    tf = min(FFN_TILE_F, d_ff)
    n_f = d_ff // tf

    def f_idx(i, f, nu):
        return jnp.where(i < nu[0], f, n_f - 1)

    ys = pl.pallas_call(
        _moe_ffn_kernel,
        grid_spec=pltpu.PrefetchScalarGridSpec(
            num_scalar_prefetch=2,
            grid=(n_tiles, n_f),
            in_specs=[pl.BlockSpec(memory_space=pl.ANY),
                      pl.BlockSpec((1, d, tf), lambda i, f, te, nu: (te[i], 0, f_idx(i, f, nu))),
                      pl.BlockSpec((1, d, tf), lambda i, f, te, nu: (te[i], 0, f_idx(i, f, nu))),
                      pl.BlockSpec((1, tf, d), lambda i, f, te, nu: (te[i], f_idx(i, f, nu), 0))],
            out_specs=pl.BlockSpec((tm, 1, d), lambda i, f, te, nu: (i, 0, 0)),
            scratch_shapes=[pltpu.VMEM((2, tm, d), F32), pltpu.SemaphoreType.DMA((2,)),
                            pltpu.VMEM((tm, d), BF16), pltpu.VMEM((tm, d), F32)]),
        out_shape=jax.ShapeDtypeStruct((n_tiles * tm, 1, d), F32),
        compiler_params=pltpu.CompilerParams(
            dimension_semantics=("arbitrary", "arbitrary"), vmem_limit_bytes=VMEM_LIMIT),
        name="moe_ffn",
    )(tile_expert, n_used, xs, w1, w3, w2)

    tc = min(COMBINE_TILE, slen)
    per_seq_c = slen // tc
    n_c = n // tc
    pos3 = pos.reshape(2 * n_c, 1, tc)
    out = pl.pallas_call(
        functools.partial(_combine_kernel, final_norm=final_norm),
        grid=(n_c,),
        in_specs=[pl.BlockSpec((1, 1, tc), lambda i: (i, 0, 0), memory_space=pltpu.SMEM),
                  pl.BlockSpec((1, 1, tc), lambda i: (i + n_c, 0, 0), memory_space=pltpu.SMEM),
                  pl.BlockSpec((tc, d), lambda i: (i, 0)),
                  pl.BlockSpec((tc, ROUTER_LANES), lambda i: (i, 0)),
                  pl.BlockSpec((1, 1, d), lambda i: (i // per_seq_c, 0, 0)),
                  pl.BlockSpec((1, d), lambda i: (0, 0)),
                  pl.BlockSpec(memory_space=pl.ANY)],
        out_specs=pl.BlockSpec((tc, d), lambda i: (i, 0)),
        out_shape=jax.ShapeDtypeStruct((n, d), F32),
        scratch_shapes=[pltpu.VMEM((2, tc, d), F32), pltpu.SemaphoreType.DMA(())],
        compiler_params=pltpu.CompilerParams(
            dimension_semantics=("arbitrary",), vmem_limit_bytes=VMEM_LIMIT),
        name="moe_combine",
    )(pos3, pos3, x2, route, gate, final_w, ys)
    return out.reshape(bsz, slen, d)


def _mix_params(l, norm_mix, w_in, w_out, pool_w, pool_scale, pool_out_norm, sgu_ws, sgu_b,
                sgu_out_norm, conv_w, conv_b, lru_wa, lru_ba, lru_wx, lru_bx, lru_lambda,
                lru_out_norm, lower_bounds, hgrn_out_norm):
    tri = jnp.tril(jnp.ones((SGU_BLOCK, SGU_BLOCK), sgu_ws.dtype))
    sgu_w = jnp.concatenate([sgu_ws[l, hh] * tri for hh in range(sgu_ws.shape[1])], axis=1)
    return {
        "norm": _row(norm_mix[l]),
        "w_in": w_in[l].astype(BF16),
        "w_out": w_out[l].astype(BF16),
        "pool_w": _block_diag(pool_w[l]).astype(BF16),
        "pool_scale": _row(pool_scale[l]),
        "pool_norm": _row(pool_out_norm[l]),
        "sgu_w": sgu_w.astype(BF16),
        "sgu_b": jnp.repeat(sgu_b[l].T, HEAD_DIM, axis=1).astype(F32),
        "sgu_norm": _row(sgu_out_norm[l]),
        "conv_w": conv_w[l].astype(F32),
        "conv_b": _row(conv_b[l]),
        "lru_w": jnp.concatenate([_block_diag(lru_wa[l]), _block_diag(lru_wx[l])], axis=1).astype(BF16),
        "lru_b": _row(jnp.concatenate([lru_ba[l], lru_bx[l]])),
        "lru_log": _row(LRU_C * jax.nn.log_sigmoid(lru_lambda[l].astype(F32))),
        "lru_norm": _row(lru_out_norm[l]),
        "hgrn_lb": _row(lower_bounds[l]),
        "hgrn_norm": _row(hgrn_out_norm[l]),
    }


def kernel(x, c, w_mod, b_mod, norm_mix, norm_ffn, norm_final, w_in, w_out, pool_w, pool_scale, pool_out_norm, sgu_ws, sgu_b, sgu_out_norm, conv_w, conv_b, lru_wa, lru_ba, lru_wx, lru_bx, lru_lambda, lru_out_norm, hgrn_gamma, hgrn_out_norm, ffn_w1, ffn_w3, ffn_w2, router_w, router_b, moe_w1, moe_w3, moe_w2):
    depth = w_mod.shape[0]
    bsz, slen, d = x.shape
    p_layers = jax.nn.softmax(hgrn_gamma.astype(F32), axis=0)
    lower_bounds = jnp.cumsum(p_layers, axis=0) - p_layers[0:1]
    mod = _modulation(c, w_mod, b_mod)
    final_w = _row(norm_final)
    for l in range(depth):
        m = mod[l].reshape(bsz, 6, 1, d)
        sh1, sc1, g1, sh2, sc2, g2 = (m[:, i] for i in range(6))
        p = _mix_params(l, norm_mix, w_in, w_out, pool_w, pool_scale, pool_out_norm, sgu_ws,
                        sgu_b, sgu_out_norm, conv_w, conv_b, lru_wa, lru_ba, lru_wx, lru_bx,
                        lru_lambda, lru_out_norm, lower_bounds, hgrn_out_norm)
        x = _token_mix(x, sh1, sc1, g1, p)
        i = l // 2
        final_norm = l == depth - 1
        if l % 2 == 0:
            x = _ffn_mix(x, sh2, sc2, g2, _row(norm_ffn[l]), ffn_w1[i].astype(BF16),
                         ffn_w3[i].astype(BF16), ffn_w2[i].astype(BF16), final_w,
                         final_norm=final_norm)
        else:
            n_e = router_w.shape[2]
            rw = jnp.pad(router_w[i].astype(F32), ((0, 0), (0, ROUTER_LANES - n_e)))
            rw_hi = rw.astype(BF16)
            rw = jnp.stack([rw_hi, (rw - rw_hi.astype(F32)).astype(BF16)])
            rb = jnp.pad(router_b[i].astype(F32), (0, ROUTER_LANES - n_e),
                         constant_values=NEG_BIG).reshape(1, ROUTER_LANES)
            x = _moe_mix(x, sh2, sc2, g2, _row(norm_ffn[l]), rw, rb, moe_w1[i].astype(BF16),
                         moe_w3[i].astype(BF16), moe_w2[i].astype(BF16), final_w,
                         final_norm=final_norm)
    return x
```

```python
import functools

import jax
import jax.numpy as jnp
from jax import lax
from jax.experimental import pallas as pl
from jax.experimental.pallas import tpu as pltpu

D_MODEL = 1024
GROUP = 256
N_IN_SLICES = 9
POOL_WINDOWS = (2, 4, 8, 16)
POOL_HIST = 16
CONV_WIDTH = 4
CONV_HIST = 8
SGU_BLOCK = 128
HEAD_DIM = 64
CHUNK = 64
LRU_C = 8.0
D_FF = 3584
N_EXPERTS = 8
EPS = 1e-6
MIN_FORGET = 1e-30
HGRN_SAFE_RANGE = 80.0
ROUTER_LANES = 128
NEG_BIG = -1e30
SUBLANES = 8

MIX_TILE = 1024
FFN_TILE_M = 1024
FFN_TILE_F = 512
MOE_TILE = 1024
GATHER_CHUNK = 1024
COMBINE_TILE = 512
VMEM_LIMIT = 56 * 1024 * 1024

BF16 = jnp.bfloat16
F32 = jnp.float32


def _dot(a, b):
    return jnp.dot(a, b, preferred_element_type=F32)


def _dot_nt(a, b):
    return lax.dot_general(a, b, (((1,), (1,)), ((), ())), preferred_element_type=F32)


def _dot_tn(a, b):
    return lax.dot_general(a, b, (((0,), (0,)), ((), ())), preferred_element_type=F32)


def _gelu(x):
    return 0.5 * x * (1.0 + jnp.tanh(0.7978845608028654 * (x + 0.044715 * (x * x * x))))


def _logistic(x):
    return 1.0 / (1.0 + jnp.exp(-x))


def _sigmoid(x):
    return 0.5 * jnp.tanh(0.5 * x) + 0.5


def _silu(x):
    return x * _sigmoid(x)


def _rms(x, g):
    return x * lax.rsqrt(jnp.mean(x * x, axis=-1, keepdims=True) + EPS) * g


def _split_dot(x, m):
    hi = x.astype(BF16)
    lo = (x - hi.astype(F32)).astype(BF16)
    return _dot(hi, m) + _dot(lo, m)


def _mod_kernel(c_ref, w_ref, b_ref, o_ref):
    o_ref[0] = _dot(_silu(c_ref[...]), w_ref[0]) + b_ref[0]


def _modulation(c, w_mod, b_mod):
    depth, d, n = w_mod.shape
    bsz = c.shape[0]
    tn = 1536
    return pl.pallas_call(
        _mod_kernel,
        grid=(depth, n // tn),
        in_specs=[
            pl.BlockSpec((bsz, d), lambda l, j: (0, 0)),
            pl.BlockSpec((1, d, tn), lambda l, j: (l, 0, j)),
            pl.BlockSpec((1, 1, tn), lambda l, j: (l, 0, j)),
        ],
        out_specs=pl.BlockSpec((1, bsz, tn), lambda l, j: (l, 0, j)),
        out_shape=jax.ShapeDtypeStruct((depth, bsz, n), F32),
        compiler_params=pltpu.CompilerParams(vmem_limit_bytes=VMEM_LIMIT),
        name="adaln_modulation",
    )(c, w_mod, b_mod.reshape(depth, 1, n))


def _mix_kernel(x_ref, sh_ref, sc_ref, gate_ref, nw_ref, win_ref, wout_ref,
                poolw_ref, poolscale_ref, poolnorm_ref,
                sguw_ref, sgub_ref, sgunorm_ref,
                convw_ref, convb_ref, lruw_ref, lrub_ref, lrulog_ref, lrunorm_ref,
                lb_ref, hgrnnorm_ref,
                o_ref,
                pool_hist, conv_hist, lru_carry, hgrn_state, scan_a, scan_b, proj_scr,
                state0_scr, o_scr, ycat_scr, fb_scr):
    tile = x_ref.shape[1]
    j = pl.program_id(1)

    @pl.when(j == 0)
    def _():
        pool_hist[...] = jnp.zeros_like(pool_hist)
        conv_hist[...] = jnp.zeros_like(conv_hist)
        lru_carry[...] = jnp.zeros_like(lru_carry)
        hgrn_state[...] = jnp.zeros_like(hgrn_state)

    x = x_ref[0]
    h = _rms(x, nw_ref[...]) * (1.0 + sc_ref[0]) + sh_ref[0]
    hb = h.astype(BF16)

    proj_scr[...] = _dot(hb, win_ref[...])

    def proj(i):
        return proj_scr[:, i * GROUP:(i + 1) * GROUP]

    lane = lax.broadcasted_iota(jnp.int32, (1, GROUP), 1)
    lane_head = lane // HEAD_DIM
    row = lax.broadcasted_iota(jnp.int32, (tile, 1), 0)

    bd_r = lax.broadcasted_iota(jnp.int32, (GROUP, GROUP), 0) // HEAD_DIM
    bd_c = lax.broadcasted_iota(jnp.int32, (GROUP, GROUP), 1) // HEAD_DIM
    bd_mask = (bd_r == bd_c).astype(F32)
    bd_ones = bd_mask.astype(BF16)

    pa = proj(0)
    ext = jnp.concatenate([pool_hist[...], pa], axis=0)
    s2 = ext + pltpu.roll(ext, 1, 0)
    s4 = s2 + pltpu.roll(s2, 2, 0)
    s8 = s4 + pltpu.roll(s4, 4, 0)
    s16 = s8 + pltpu.roll(s8, 8, 0)
    pool_hist[...] = pa[tile - POOL_HIST:, :]
    wsum = jnp.where(lane_head == 0, s2, jnp.where(lane_head == 1, s4,
                     jnp.where(lane_head == 2, s8, s16)))[POOL_HIST:, :]
    win = jnp.where(lane_head == 0, 2.0, jnp.where(lane_head == 1, 4.0,
                    jnp.where(lane_head == 2, 8.0, 16.0)))
    pos = (row + j * tile + 1).astype(F32)
    d = wsum / jnp.minimum(pos, win) - pa
    ya = _dot(d.astype(BF16), poolw_ref[...]) * poolscale_ref[...]
    ya = _rms(ya, poolnorm_ref[...])

    u = _gelu(proj(1))
    vf = _gelu(proj(2))
    mu = jnp.mean(vf, axis=-1, keepdims=True)
    vc = vf - mu
    var = jnp.mean(vc * vc, axis=-1, keepdims=True)
    vn = vc * lax.rsqrt(var + EPS)
    zs = []
    for b in range(tile // SGU_BLOCK):
        vb = vn[b * SGU_BLOCK:(b + 1) * SGU_BLOCK, :]
        stack = jnp.concatenate(
            [jnp.where(lane_head == hh, vb, 0.0) for hh in range(GROUP // HEAD_DIM)], axis=0)
        zs.append(_dot(sguw_ref[...], stack.astype(BF16)) + sgub_ref[...])
    z = jnp.concatenate(zs, axis=0)
    yb = _rms(u * z, sgunorm_ref[...])

    px = proj(3)
    ext = jnp.concatenate([conv_hist[...], px], axis=0)
    xc = (convb_ref[...] + convw_ref[3:4, :] * ext + convw_ref[2:3, :] * pltpu.roll(ext, 1, 0)
          + convw_ref[1:2, :] * pltpu.roll(ext, 2, 0) + convw_ref[0:1, :] * pltpu.roll(ext, 3, 0))
    xc = xc[CONV_HIST:, :]
    conv_hist[...] = px[tile - CONV_HIST:, :]
    gates = _dot(xc.astype(BF16), lruw_ref[...]) + lrub_ref[...]
    r = _sigmoid(gates[:, :GROUP])
    ig = _sigmoid(gates[:, GROUP:])
    log_a = r * lrulog_ref[...]
    a = jnp.exp(log_a)
    bb = jnp.sqrt(jnp.maximum(1.0 - jnp.exp(2.0 * log_a), 0.0)) * (ig * xc)
    a3 = a.reshape(tile // SUBLANES, SUBLANES, GROUP)
    b3 = bb.reshape(tile // SUBLANES, SUBLANES, GROUP)
    sub = lax.broadcasted_iota(jnp.int32, (1, SUBLANES, 1), 1)
    for s in (1, 2, 4):
        keep = sub >= s
        b3 = jnp.where(keep, a3 * pltpu.roll(b3, s, 1) + b3, b3)
        a3 = jnp.where(keep, a3 * pltpu.roll(a3, s, 1), a3)
    scan_a[...] = a3.reshape(tile, GROUP)
    scan_b[...] = b3.reshape(tile, GROUP)

    def lru_step(g, carry):
        r0 = pl.multiple_of(g * SUBLANES, SUBLANES)
        h8 = scan_a[pl.ds(r0, SUBLANES), :] * carry + scan_b[pl.ds(r0, SUBLANES), :]
        scan_b[pl.ds(r0, SUBLANES), :] = h8
        return h8[SUBLANES - 1:SUBLANES, :]

    carry = lax.fori_loop(0, tile // SUBLANES, lru_step, lru_carry[0:1, :], unroll=True)
    lru_carry[...] = jnp.broadcast_to(carry, lru_carry.shape)
    yc = _rms(_gelu(proj(4)) * scan_b[...], lrunorm_ref[...])

    lb = lb_ref[...]
    qf = _silu(proj(5))
    sig = _logistic(proj(6))
    log_f = jnp.log(jnp.maximum(lb + (1.0 - lb) * sig, MIN_FORGET))
    kk = (1.0 - lb) * (1.0 - sig)
    vv = proj(7)
    csub = row % CHUNK
    bcum = log_f
    for s in (1, 2, 4, 8, 16, 32):
        bcum = jnp.where(csub >= s, bcum + pltpu.roll(bcum, s, 0), bcum)
    t_idx = lax.broadcasted_iota(jnp.int32, (CHUNK, GROUP), 0)
    s_idx = lax.broadcasted_iota(jnp.int32, (CHUNK, GROUP), 1) % HEAD_DIM
    causal = (t_idx >= s_idx).astype(F32)
    n_heads = GROUP // HEAD_DIM
    outs = []
    span = jnp.zeros((1, GROUP), F32)
    state = hgrn_state[...]
    state0_scr[...] = state
    for c in range(tile // CHUNK):
        sl = slice(c * CHUNK, (c + 1) * CHUNK)
        bc = bcum[sl]
        bmid = bc[CHUNK // 2 - 1:CHUNK // 2, :]
        blast = bc[CHUNK - 1:CHUNK, :]
        span = jnp.maximum(span, jnp.maximum(-bmid, bmid - blast))
        qm = qf[sl] * jnp.exp(bc - bmid)
        km = kk[sl] * jnp.exp(bmid - bc)
        qe = qm * jnp.exp(bmid)
        kl = km * jnp.exp(blast - bmid)
        vcb = vv[sl].astype(BF16)
        k_bd = (jnp.concatenate([km] * n_heads, axis=0) * bd_mask).astype(BF16)
        scores = _dot_nt(qm.astype(BF16), k_bd) * causal
        v_bd = (jnp.concatenate([vv[sl]] * n_heads, axis=0) * bd_mask).astype(BF16)
        outs.append(_dot(scores.astype(BF16), v_bd) + _dot_nt(qe.astype(BF16), state.astype(BF16)))
        state = state * jnp.exp(blast) + _dot_tn(vcb, kl.astype(BF16)) * bd_mask
    out_gate = _silu(proj(8))

    def hgrn_out(o):
        ms = _split_dot(o * o, bd_ones) * (1.0 / HEAD_DIM)
        return o * lax.rsqrt(ms + EPS) * hgrnnorm_ref[...] * out_gate

    yd = hgrn_out(jnp.concatenate(outs, axis=0))
    hgrn_state[...] = state
    ycat_scr[...] = jnp.concatenate(
        [ya.astype(BF16), yb.astype(BF16), yc.astype(BF16), yd.astype(BF16)], axis=1)
    o_ref[0] = x + gate_ref[0] * _dot(ycat_scr[...], wout_ref[...])

    @pl.when(jnp.max(span) > HGRN_SAFE_RANGE)
    def _():
        fb_scr[0] = qf
        fb_scr[1] = kk
        fb_scr[2] = vv
        fb_scr[3] = bcum
        t_row = lax.broadcasted_iota(jnp.int32, (CHUNK, 1), 0)

        def chunk_body(c, st):
            r0 = pl.multiple_of(c * CHUNK, CHUNK)
            q_c = fb_scr[0, pl.ds(r0, CHUNK), :]
            k_c = fb_scr[1, pl.ds(r0, CHUNK), :]
            v_c = fb_scr[2, pl.ds(r0, CHUNK), :]
            b_c = fb_scr[3, pl.ds(r0, CHUNK), :]
            b_end = fb_scr[3, pl.ds(r0 + CHUNK - 1, 1), :]

            def pair_body(s, acc):
                k_s = fb_scr[1, pl.ds(r0 + s, 1), :]
                v_s = fb_scr[2, pl.ds(r0 + s, 1), :]
                b_s = fb_scr[3, pl.ds(r0 + s, 1), :]
                w = jnp.where(t_row >= s, jnp.exp(jnp.minimum(b_c - b_s, 0.0)), 0.0)
                return acc + _split_dot(q_c * w * k_s, bd_ones) * v_s

            o_intra = lax.fori_loop(0, CHUNK, pair_body, jnp.zeros((CHUNK, GROUP), F32))
            o_inter = _dot_nt((q_c * jnp.exp(b_c)).astype(BF16), st.astype(BF16))
            o_scr[pl.ds(r0, CHUNK), :] = o_inter + o_intra
            k_end = k_c * jnp.exp(b_end - b_c)
            return st * jnp.exp(b_end) + _dot_tn(v_c.astype(BF16), k_end.astype(BF16)) * bd_mask

        hgrn_state[...] = lax.fori_loop(0, tile // CHUNK, chunk_body, state0_scr[...])
        ycat_scr[:, 3 * GROUP:] = hgrn_out(o_scr[...]).astype(BF16)
        o_ref[0] = x + gate_ref[0] * _dot(ycat_scr[...], wout_ref[...])


def _row(v):
    return v.reshape(1, -1).astype(F32)


def _block_diag(w):
    n, a, b = w.shape
    eye = jnp.eye(n, dtype=w.dtype)
    return (eye[:, None, :, None] * w[:, :, None, :]).reshape(n * a, n * b)


def _token_mix(x, sh, sc, gate, p):
    bsz, slen, d = x.shape
    tile = min(MIX_TILE, slen)
    full = lambda a: pl.BlockSpec(a.shape, lambda b, j: (0,) * a.ndim)
    per_batch = pl.BlockSpec((1, 1, d), lambda b, j: (b, 0, 0))
    x_spec = pl.BlockSpec((1, tile, d), lambda b, j: (b, j, 0))
    params = [p["norm"], p["w_in"], p["w_out"], p["pool_w"], p["pool_scale"], p["pool_norm"],
              p["sgu_w"], p["sgu_b"], p["sgu_norm"], p["conv_w"], p["conv_b"], p["lru_w"],
              p["lru_b"], p["lru_log"], p["lru_norm"], p["hgrn_lb"], p["hgrn_norm"]]
    return pl.pallas_call(
        _mix_kernel,
        grid=(bsz, slen // tile),
        in_specs=[x_spec, per_batch, per_batch, per_batch] + [full(a) for a in params],
        out_specs=x_spec,
        out_shape=jax.ShapeDtypeStruct(x.shape, x.dtype),
        scratch_shapes=[
            pltpu.VMEM((POOL_HIST, GROUP), F32),
            pltpu.VMEM((CONV_HIST, GROUP), F32),
            pltpu.VMEM((SUBLANES, GROUP), F32),
            pltpu.VMEM((GROUP, GROUP), F32),
            pltpu.VMEM((tile, GROUP), F32),
            pltpu.VMEM((tile, GROUP), F32),
            pltpu.VMEM((tile, N_IN_SLICES * GROUP), F32),
            pltpu.VMEM((GROUP, GROUP), F32),
            pltpu.VMEM((tile, GROUP), F32),
            pltpu.VMEM((tile, D_MODEL), BF16),
            pltpu.VMEM((4, tile, GROUP), F32),
        ],
        compiler_params=pltpu.CompilerParams(
            dimension_semantics=("arbitrary", "arbitrary"), vmem_limit_bytes=VMEM_LIMIT),
        name="token_mix",
    )(x, sh, sc, gate, *params)


def _ffn_kernel(x_ref, sh_ref, sc_ref, gate_ref, nw_ref, w1_ref, w3_ref, w2_ref, fw_ref,
                o_ref, h_scr, acc, *, final_norm):
    f = pl.program_id(1)

    @pl.when(f == 0)
    def _():
        h = _rms(x_ref[...], nw_ref[...]) * (1.0 + sc_ref[0]) + sh_ref[0]
        h_scr[...] = h.astype(BF16)
        acc[...] = jnp.zeros_like(acc)

    hb = h_scr[...]
    act = _silu(_dot(hb, w1_ref[...])) * _dot(hb, w3_ref[...])
    acc[...] += _dot(act.astype(BF16), w2_ref[...])

    @pl.when(f == pl.num_programs(1) - 1)
    def _():
        out = x_ref[...] + gate_ref[0] * acc[...]
        if final_norm:
            out = _rms(out, fw_ref[...])
        o_ref[...] = out


def _route_kernel(x_ref, sh_ref, sc_ref, nw_ref, rw_ref, rb_ref, h_ref, route_ref):
    h = _rms(x_ref[...], nw_ref[...]) * (1.0 + sc_ref[0]) + sh_ref[0]
    h_ref[...] = h
    h_hi = h.astype(BF16)
    h_lo = (h - h_hi.astype(F32)).astype(BF16)
    logits = (_dot(h_hi, rw_ref[0]) + _dot(h_lo, rw_ref[0]) + _dot(h_hi, rw_ref[1])) + rb_ref[...]
    lane = lax.broadcasted_iota(jnp.int32, logits.shape, 1)
    v1 = jnp.max(logits, axis=-1, keepdims=True)
    i1 = jnp.min(jnp.where(logits == v1, lane, ROUTER_LANES), axis=-1, keepdims=True)
    rest = jnp.where(lane == i1, NEG_BIG, logits)
    v2 = jnp.max(rest, axis=-1, keepdims=True)
    i2 = jnp.min(jnp.where(rest == v2, lane, ROUTER_LANES), axis=-1, keepdims=True)
    p2 = jnp.exp(v2 - v1)
    w_top = 1.0 / (1.0 + p2)
    route_ref[...] = jnp.where(lane == 0, i1.astype(F32), jnp.where(lane == 1, i2.astype(F32),
                               jnp.where(lane == 2, w_top, jnp.where(lane == 3, p2 * w_top, 0.0))))


def _gather_kernel(pad_ref, nu_ref, p1_ref, p2_ref, h_ref, xs_ref, zeros, sem, zsem):
    i = pl.program_id(0)
    chunk = h_ref.shape[0]
    tile = zeros.shape[0]
    n_tiles = xs_ref.shape[0] // tile

    @pl.when(i == 0)
    def _():
        zeros[...] = jnp.zeros_like(zeros)
        for e in range(pad_ref.shape[0]):
            cp = pltpu.make_async_copy(zeros, xs_ref.at[pl.ds(pad_ref[e], tile)], zsem)
            cp.start()
            cp.wait()

        def tail(t, carry):
            cp = pltpu.make_async_copy(
                zeros, xs_ref.at[pl.ds(pl.multiple_of(t * tile, tile), tile)], zsem)
            cp.start()
            cp.wait()
            return carry

        lax.fori_loop(nu_ref[0], n_tiles, tail, 0)

    def issue(r, carry):
        pltpu.make_async_copy(h_ref.at[pl.ds(r, 1)], xs_ref.at[p1_ref[0, 0, r]], sem).start()
        pltpu.make_async_copy(h_ref.at[pl.ds(r, 1)], xs_ref.at[p2_ref[0, 0, r]], sem).start()
        return carry

    lax.fori_loop(0, chunk, issue, 0, unroll=8)
    for _ in range(2):
        pltpu.make_async_copy(h_ref, xs_ref.at[pl.ds(0, chunk), 0], sem).wait()


def _moe_ffn_kernel(te_ref, nu_ref, xs_ref, w1_ref, w3_ref, w2_ref, ys_ref, xbuf, xsem, h_scr, acc):
    i = pl.program_id(0)
    f = pl.program_id(1)
    tm = xbuf.shape[1]

    def slab(t, slot):
        rows = pl.ds(pl.multiple_of(t * tm, tm), tm)
        return pltpu.make_async_copy(xs_ref.at[rows, 0], xbuf.at[slot], xsem.at[slot])

    @pl.when((i == 0) & (f == 0))
    def _():
        slab(0, 0).start()

    @pl.when(i < nu_ref[0])
    def _():
        @pl.when(f == 0)
        def _():
            slot = i % 2

            @pl.when(i + 1 < nu_ref[0])
            def _():
                slab(i + 1, 1 - slot).start()

            slab(i, slot).wait()
            h_scr[...] = xbuf[slot].astype(BF16)
            acc[...] = jnp.zeros_like(acc)

        hb = h_scr[...]
        act = _silu(_dot(hb, w1_ref[0])) * _dot(hb, w3_ref[0])
        acc[...] += _dot(act.astype(BF16), w2_ref[0])

        @pl.when(f == pl.num_programs(1) - 1)
        def _():
            ys_ref[...] = acc[...].reshape(ys_ref.shape)

    @pl.when((i >= nu_ref[0]) & (f == 0))
    def _():
        ys_ref[...] = jnp.zeros_like(ys_ref)


def _combine_kernel(p1_ref, p2_ref, x_ref, route_ref, gate_ref, fw_ref, ys_ref, o_ref,
                    ybuf, sem, *, final_norm):
    tm = x_ref.shape[0]

    def issue(r, carry):
        pltpu.make_async_copy(ys_ref.at[p1_ref[0, 0, r]], ybuf.at[0, pl.ds(r, 1)], sem).start()
        pltpu.make_async_copy(ys_ref.at[p2_ref[0, 0, r]], ybuf.at[1, pl.ds(r, 1)], sem).start()
        return carry

    lax.fori_loop(0, tm, issue, 0, unroll=8)
    for s in range(2):
        pltpu.make_async_copy(ys_ref.at[pl.ds(0, tm), 0], ybuf.at[s], sem).wait()
    route = route_ref[...]
    y = route[:, 2:3] * ybuf[0] + route[:, 3:4] * ybuf[1]
    out = x_ref[...] + gate_ref[0] * y
    if final_norm:
        out = _rms(out, fw_ref[...])
    o_ref[...] = out


def _ffn_mix(x, sh, sc, gate, norm_w, w1, w3, w2, final_w, *, final_norm):
    bsz, slen, d = x.shape
    tm = min(FFN_TILE_M, slen)
    per_seq = slen // tm
    d_ff = w1.shape[1]
    tf = min(FFN_TILE_F, d_ff)
    x2 = x.reshape(bsz * slen, d)
    x_spec = pl.BlockSpec((tm, d), lambda i, f: (i, 0))
    per_batch = pl.BlockSpec((1, 1, d), lambda i, f: (i // per_seq, 0, 0))
    row_spec = pl.BlockSpec((1, d), lambda i, f: (0, 0))
    out = pl.pallas_call(
        functools.partial(_ffn_kernel, final_norm=final_norm),
        grid=(bsz * per_seq, d_ff // tf),
        in_specs=[x_spec, per_batch, per_batch, per_batch, row_spec,
                  pl.BlockSpec((d, tf), lambda i, f: (0, f)),
                  pl.BlockSpec((d, tf), lambda i, f: (0, f)),
                  pl.BlockSpec((tf, d), lambda i, f: (f, 0)),
                  row_spec],
        out_specs=x_spec,
        out_shape=jax.ShapeDtypeStruct(x2.shape, x2.dtype),
        scratch_shapes=[pltpu.VMEM((tm, d), BF16), pltpu.VMEM((tm, d), F32)],
        compiler_params=pltpu.CompilerParams(
            dimension_semantics=("arbitrary", "arbitrary"), vmem_limit_bytes=VMEM_LIMIT),
        name="ffn_mix",
    )(x2, sh, sc, gate, norm_w, w1, w3, w2, final_w)
    return out.reshape(bsz, slen, d)


def _moe_mix(x, sh, sc, gate, norm_w, rw, rb, w1, w3, w2, final_w, *, final_norm):
    bsz, slen, d = x.shape
    n = bsz * slen
    n_e, _, d_ff = w1.shape
    x2 = x.reshape(n, d)

    tr = min(FFN_TILE_M, slen)
    per_seq = slen // tr
    per_batch = pl.BlockSpec((1, 1, d), lambda i: (i // per_seq, 0, 0))
    h3, route = pl.pallas_call(
        _route_kernel,
        grid=(n // tr,),
        in_specs=[pl.BlockSpec((tr, d), lambda i: (i, 0)), per_batch, per_batch,
                  pl.BlockSpec((1, d), lambda i: (0, 0)),
                  pl.BlockSpec(rw.shape, lambda i: (0, 0, 0)),
                  pl.BlockSpec(rb.shape, lambda i: (0, 0))],
        out_specs=[pl.BlockSpec((tr, d), lambda i: (i, 0)),
                   pl.BlockSpec((tr, ROUTER_LANES), lambda i: (i, 0))],
        out_shape=[jax.ShapeDtypeStruct((n, d), F32),
                   jax.ShapeDtypeStruct((n, ROUTER_LANES), F32)],
        compiler_params=pltpu.CompilerParams(vmem_limit_bytes=VMEM_LIMIT),
        name="moe_route",
    )(x2, sh, sc, norm_w, rw, rb)

    tm = min(MOE_TILE, n)
    n_tiles = 2 * n // tm + n_e
    rows = (n_tiles + 1) * tm
    e_flat = jnp.concatenate([route[:, 0], route[:, 1]]).astype(jnp.int32)
    onehot = (e_flat[:, None] == jnp.arange(n_e, dtype=jnp.int32)[None, :]).astype(jnp.int32)
    csum = jnp.cumsum(onehot, axis=0)
    counts = csum[-1]
    padded = ((counts + tm - 1) // tm) * tm
    gstart = jnp.cumsum(padded) - padded
    pos = jnp.sum(onehot * (gstart[None, :] + csum - 1), axis=1).astype(jnp.int32)
    pad_start = (gstart + counts).astype(jnp.int32)
    tile_end = jnp.cumsum(padded // tm)
    n_used = tile_end[-1:].astype(jnp.int32)
    tile_ids = jnp.minimum(jnp.arange(n_tiles, dtype=jnp.int32), n_used[0] - 1)
    tile_expert = jnp.sum((tile_ids[:, None] >= tile_end[None, :]).astype(jnp.int32), axis=1)

    chunk = min(GATHER_CHUNK, n)
    per_slot = n // chunk
    xs = pl.pallas_call(
        _gather_kernel,
        grid_spec=pltpu.PrefetchScalarGridSpec(
            num_scalar_prefetch=2,
            grid=(per_slot,),
            in_specs=[pl.BlockSpec((1, 1, chunk), lambda i, pad, nu: (i, 0, 0),
                                   memory_space=pltpu.SMEM),
                      pl.BlockSpec((1, 1, chunk), lambda i, pad, nu: (i + per_slot, 0, 0),
                                   memory_space=pltpu.SMEM),
                      pl.BlockSpec((chunk, d), lambda i, pad, nu: (i, 0))],
            out_specs=pl.BlockSpec(memory_space=pl.ANY),
            scratch_shapes=[pltpu.VMEM((tm, 1, d), F32), pltpu.SemaphoreType.DMA(()),
                            pltpu.SemaphoreType.DMA(())]),
        out_shape=jax.ShapeDtypeStruct((rows, 1, d), F32),
        compiler_params=pltpu.CompilerParams(
            dimension_semantics=("arbitrary",), vmem_limit_bytes=VMEM_LIMIT),
        name="moe_gather",
    )(pad_start, n_used, pos.reshape(2 * per_slot, 1, chunk), pos.reshape(2 * per_slot, 1, chunk), h3)

    tf = min(FFN_TILE_F, d_ff)
    n_f = d_ff // tf

    def f_idx(i, f, nu):
        return jnp.where(i < nu[0], f, n_f - 1)

    ys = pl.pallas_call(
        _moe_ffn_kernel,
        grid_spec=pltpu.PrefetchScalarGridSpec(
            num_scalar_prefetch=2,
            grid=(n_tiles, n_f),
            in_specs=[pl.BlockSpec(memory_space=pl.ANY),
                      pl.BlockSpec((1, d, tf), lambda i, f, te, nu: (te[i], 0, f_idx(i, f, nu))),
                      pl.BlockSpec((1, d, tf), lambda i, f, te, nu: (te[i], 0, f_idx(i, f, nu))),
                      pl.BlockSpec((1, tf, d), lambda i, f, te, nu: (te[i], f_idx(i, f, nu), 0))],
            out_specs=pl.BlockSpec((tm, 1, d), lambda i, f, te, nu: (i, 0, 0)),
            scratch_shapes=[pltpu.VMEM((2, tm, d), F32), pltpu.SemaphoreType.DMA((2,)),
                            pltpu.VMEM((tm, d), BF16), pltpu.VMEM((tm, d), F32)]),
        out_shape=jax.ShapeDtypeStruct((n_tiles * tm, 1, d), F32),
        compiler_params=pltpu.CompilerParams(
            dimension_semantics=("arbitrary", "arbitrary"), vmem_limit_bytes=VMEM_LIMIT),
        name="moe_ffn",
    )(tile_expert, n_used, xs, w1, w3, w2)

    tc = min(COMBINE_TILE, slen)
    per_seq_c = slen // tc
    n_c = n // tc
    pos3 = pos.reshape(2 * n_c, 1, tc)
    out = pl.pallas_call(
        functools.partial(_combine_kernel, final_norm=final_norm),
        grid=(n_c,),
        in_specs=[pl.BlockSpec((1, 1, tc), lambda i: (i, 0, 0), memory_space=pltpu.SMEM),
                  pl.BlockSpec((1, 1, tc), lambda i: (i + n_c, 0, 0), memory_space=pltpu.SMEM),
                  pl.BlockSpec((tc, d), lambda i: (i, 0)),
                  pl.BlockSpec((tc, ROUTER_LANES), lambda i: (i, 0)),
                  pl.BlockSpec((1, 1, d), lambda i: (i // per_seq_c, 0, 0)),
                  pl.BlockSpec((1, d), lambda i: (0, 0)),
                  pl.BlockSpec(memory_space=pl.ANY)],
        out_specs=pl.BlockSpec((tc, d), lambda i: (i, 0)),
        out_shape=jax.ShapeDtypeStruct((n, d), F32),
        scratch_shapes=[pltpu.VMEM((2, tc, d), F32), pltpu.SemaphoreType.DMA(())],
        compiler_params=pltpu.CompilerParams(
            dimension_semantics=("arbitrary",), vmem_limit_bytes=VMEM_LIMIT),
        name="moe_combine",
    )(pos3, pos3, x2, route, gate, final_w, ys)
    return out.reshape(bsz, slen, d)


def _mix_params(l, norm_mix, w_in, w_out, pool_w, pool_scale, pool_out_norm, sgu_ws, sgu_b,
                sgu_out_norm, conv_w, conv_b, lru_wa, lru_ba, lru_wx, lru_bx, lru_lambda,
                lru_out_norm, lower_bounds, hgrn_out_norm):
    tri = jnp.tril(jnp.ones((SGU_BLOCK, SGU_BLOCK), sgu_ws.dtype))
    sgu_w = jnp.concatenate([sgu_ws[l, hh] * tri for hh in range(sgu_ws.shape[1])], axis=1)
    return {
        "norm": _row(norm_mix[l]),
        "w_in": w_in[l].astype(BF16),
        "w_out": w_out[l].astype(BF16),
        "pool_w": _block_diag(pool_w[l]).astype(BF16),
        "pool_scale": _row(pool_scale[l]),
        "pool_norm": _row(pool_out_norm[l]),
        "sgu_w": sgu_w.astype(BF16),
        "sgu_b": jnp.repeat(sgu_b[l].T, HEAD_DIM, axis=1).astype(F32),
        "sgu_norm": _row(sgu_out_norm[l]),
        "conv_w": conv_w[l].astype(F32),
        "conv_b": _row(conv_b[l]),
        "lru_w": jnp.concatenate([_block_diag(lru_wa[l]), _block_diag(lru_wx[l])], axis=1).astype(BF16),
        "lru_b": _row(jnp.concatenate([lru_ba[l], lru_bx[l]])),
        "lru_log": _row(LRU_C * jax.nn.log_sigmoid(lru_lambda[l].astype(F32))),
        "lru_norm": _row(lru_out_norm[l]),
        "hgrn_lb": _row(lower_bounds[l]),
        "hgrn_norm": _row(hgrn_out_norm[l]),
    }


def kernel(x, c, w_mod, b_mod, norm_mix, norm_ffn, norm_final, w_in, w_out, pool_w, pool_scale, pool_out_norm, sgu_ws, sgu_b, sgu_out_norm, conv_w, conv_b, lru_wa, lru_ba, lru_wx, lru_bx, lru_lambda, lru_out_norm, hgrn_gamma, hgrn_out_norm, ffn_w1, ffn_w3, ffn_w2, router_w, router_b, moe_w1, moe_w3, moe_w2):
    depth = w_mod.shape[0]
    bsz, slen, d = x.shape
    p_layers = jax.nn.softmax(hgrn_gamma.astype(F32), axis=0)
    lower_bounds = jnp.cumsum(p_layers, axis=0) - p_layers[0:1]
    mod = _modulation(c, w_mod, b_mod)
    final_w = _row(norm_final)
    for l in range(depth):
        m = mod[l].reshape(bsz, 6, 1, d)
        sh1, sc1, g1, sh2, sc2, g2 = (m[:, i] for i in range(6))
        p = _mix_params(l, norm_mix, w_in, w_out, pool_w, pool_scale, pool_out_norm, sgu_ws,
                        sgu_b, sgu_out_norm, conv_w, conv_b, lru_wa, lru_ba, lru_wx, lru_bx,
                        lru_lambda, lru_out_norm, lower_bounds, hgrn_out_norm)
        x = _token_mix(x, sh1, sc1, g1, p)
        i = l // 2
        final_norm = l == depth - 1
        if l % 2 == 0:
            x = _ffn_mix(x, sh2, sc2, g2, _row(norm_ffn[l]), ffn_w1[i].astype(BF16),
                         ffn_w3[i].astype(BF16), ffn_w2[i].astype(BF16), final_w,
                         final_norm=final_norm)
        else:
            n_e = router_w.shape[2]
            rw = jnp.pad(router_w[i].astype(F32), ((0, 0), (0, ROUTER_LANES - n_e)))
            rw_hi = rw.astype(BF16)
            rw = jnp.stack([rw_hi, (rw - rw_hi.astype(F32)).astype(BF16)])
            rb = jnp.pad(router_b[i].astype(F32), (0, ROUTER_LANES - n_e),
                         constant_values=NEG_BIG).reshape(1, ROUTER_LANES)
            x = _moe_mix(x, sh2, sc2, g2, _row(norm_ffn[l]), rw, rb, moe_w1[i].astype(BF16),
                         moe_w3[i].astype(BF16), moe_w2[i].astype(BF16), final_w,
                         final_norm=final_norm)
    return x
```

```python
import functools

import jax
import jax.numpy as jnp
from jax import lax
from jax.experimental import pallas as pl
from jax.experimental.pallas import tpu as pltpu

D_MODEL = 1024
GROUP = 256
N_IN_SLICES = 9
POOL_WINDOWS = (2, 4, 8, 16)
POOL_HIST = 16
CONV_WIDTH = 4
CONV_HIST = 8
SGU_BLOCK = 128
HEAD_DIM = 64
CHUNK = 64
LRU_C = 8.0
D_FF = 3584
N_EXPERTS = 8
EPS = 1e-6
MIN_FORGET = 1e-30
HGRN_SAFE_RANGE = 80.0
ROUTER_LANES = 128
NEG_BIG = -1e30
SUBLANES = 8

MIX_TILE = 1024
FFN_TILE_M = 1024
FFN_TILE_F = 512
MOE_TILE = 1024
GATHER_CHUNK = 1024
COMBINE_TILE = 512
VMEM_LIMIT = 56 * 1024 * 1024

BF16 = jnp.bfloat16
F32 = jnp.float32


def _dot(a, b):
    return jnp.dot(a, b, preferred_element_type=F32)


def _dot_nt(a, b):
    return lax.dot_general(a, b, (((1,), (1,)), ((), ())), preferred_element_type=F32)


def _dot_tn(a, b):
    return lax.dot_general(a, b, (((0,), (0,)), ((), ())), preferred_element_type=F32)


def _gelu(x):
    return 0.5 * x * (1.0 + jnp.tanh(0.7978845608028654 * (x + 0.044715 * (x * x * x))))


def _logistic(x):
    return 1.0 / (1.0 + jnp.exp(-x))


def _sigmoid(x):
    return 0.5 * jnp.tanh(0.5 * x) + 0.5


def _silu(x):
    return x * _sigmoid(x)


def _rms(x, g):
    return x * lax.rsqrt(jnp.mean(x * x, axis=-1, keepdims=True) + EPS) * g


def _split_dot(x, m):
    hi = x.astype(BF16)
    lo = (x - hi.astype(F32)).astype(BF16)
    return _dot(hi, m) + _dot(lo, m)


def _mod_kernel(c_ref, w_ref, b_ref, o_ref):
    o_ref[0] = _dot(_silu(c_ref[...]), w_ref[0]) + b_ref[0]


def _modulation(c, w_mod, b_mod):
    depth, d, n = w_mod.shape
    bsz = c.shape[0]
    tn = 1536
    return pl.pallas_call(
        _mod_kernel,
        grid=(depth, n // tn),
        in_specs=[
            pl.BlockSpec((bsz, d), lambda l, j: (0, 0)),
            pl.BlockSpec((1, d, tn), lambda l, j: (l, 0, j)),
            pl.BlockSpec((1, 1, tn), lambda l, j: (l, 0, j)),
        ],
        out_specs=pl.BlockSpec((1, bsz, tn), lambda l, j: (l, 0, j)),
        out_shape=jax.ShapeDtypeStruct((depth, bsz, n), F32),
        compiler_params=pltpu.CompilerParams(vmem_limit_bytes=VMEM_LIMIT),
        name="adaln_modulation",
    )(c, w_mod, b_mod.reshape(depth, 1, n))


def _mix_kernel(x_ref, sh_ref, sc_ref, gate_ref, nw_ref, win_ref, wout_ref,
                poolw_ref, poolscale_ref, poolnorm_ref,
                sguw_ref, sgub_ref, sgunorm_ref,
                convw_ref, convb_ref, lruw_ref, lrub_ref, lrulog_ref, lrunorm_ref,
                lb_ref, hgrnnorm_ref,
                o_ref,
                pool_hist, conv_hist, lru_carry, hgrn_state, scan_a, scan_b, proj_scr,
                state0_scr, o_scr, ycat_scr, fb_scr):
    tile = x_ref.shape[1]
    j = pl.program_id(1)

    @pl.when(j == 0)
    def _():
        pool_hist[...] = jnp.zeros_like(pool_hist)
        conv_hist[...] = jnp.zeros_like(conv_hist)
        lru_carry[...] = jnp.zeros_like(lru_carry)
        hgrn_state[...] = jnp.zeros_like(hgrn_state)

    x = x_ref[0]
    h = _rms(x, nw_ref[...]) * (1.0 + sc_ref[0]) + sh_ref[0]
    hb = h.astype(BF16)

    proj_scr[...] = _dot(hb, win_ref[...])

    def proj(i):
        return proj_scr[:, i * GROUP:(i + 1) * GROUP]

    lane = lax.broadcasted_iota(jnp.int32, (1, GROUP), 1)
    lane_head = lane // HEAD_DIM
    row = lax.broadcasted_iota(jnp.int32, (tile, 1), 0)

    bd_r = lax.broadcasted_iota(jnp.int32, (GROUP, GROUP), 0) // HEAD_DIM
    bd_c = lax.broadcasted_iota(jnp.int32, (GROUP, GROUP), 1) // HEAD_DIM
    bd_mask = (bd_r == bd_c).astype(F32)
    bd_ones = bd_mask.astype(BF16)

    pa = proj(0)
    ext = jnp.concatenate([pool_hist[...], pa], axis=0)
    s2 = ext + pltpu.roll(ext, 1, 0)
    s4 = s2 + pltpu.roll(s2, 2, 0)
    s8 = s4 + pltpu.roll(s4, 4, 0)
    s16 = s8 + pltpu.roll(s8, 8, 0)
    pool_hist[...] = pa[tile - POOL_HIST:, :]
    wsum = jnp.where(lane_head == 0, s2, jnp.where(lane_head == 1, s4,
                     jnp.where(lane_head == 2, s8, s16)))[POOL_HIST:, :]
    win = jnp.where(lane_head == 0, 2.0, jnp.where(lane_head == 1, 4.0,
                    jnp.where(lane_head == 2, 8.0, 16.0)))
    pos = (row + j * tile + 1).astype(F32)
    d = wsum / jnp.minimum(pos, win) - pa
    ya = _dot(d.astype(BF16), poolw_ref[...]) * poolscale_ref[...]
    ya = _rms(ya, poolnorm_ref[...])

    u = _gelu(proj(1))
    vf = _gelu(proj(2))
    mu = jnp.mean(vf, axis=-1, keepdims=True)
    vc = vf - mu
    var = jnp.mean(vc * vc, axis=-1, keepdims=True)
    vn = vc * lax.rsqrt(var + EPS)
    zs = []
    for b in range(tile // SGU_BLOCK):
        vb = vn[b * SGU_BLOCK:(b + 1) * SGU_BLOCK, :]
        stack = jnp.concatenate(
            [jnp.where(lane_head == hh, vb, 0.0) for hh in range(GROUP // HEAD_DIM)], axis=0)
        zs.append(_dot(sguw_ref[...], stack.astype(BF16)) + sgub_ref[...])
    z = jnp.concatenate(zs, axis=0)
    yb = _rms(u * z, sgunorm_ref[...])

    px = proj(3)
    ext = jnp.concatenate([conv_hist[...], px], axis=0)
    xc = (convb_ref[...] + convw_ref[3:4, :] * ext + convw_ref[2:3, :] * pltpu.roll(ext, 1, 0)
          + convw_ref[1:2, :] * pltpu.roll(ext, 2, 0) + convw_ref[0:1, :] * pltpu.roll(ext, 3, 0))
    xc = xc[CONV_HIST:, :]
    conv_hist[...] = px[tile - CONV_HIST:, :]
    gates = _dot(xc.astype(BF16), lruw_ref[...]) + lrub_ref[...]
    r = _sigmoid(gates[:, :GROUP])
    ig = _sigmoid(gates[:, GROUP:])
    log_a = r * lrulog_ref[...]
    a = jnp.exp(log_a)
    bb = jnp.sqrt(jnp.maximum(1.0 - jnp.exp(2.0 * log_a), 0.0)) * (ig * xc)
    a3 = a.reshape(tile // SUBLANES, SUBLANES, GROUP)
    b3 = bb.reshape(tile // SUBLANES, SUBLANES, GROUP)
    sub = lax.broadcasted_iota(jnp.int32, (1, SUBLANES, 1), 1)
    for s in (1, 2, 4):
        keep = sub >= s
        b3 = jnp.where(keep, a3 * pltpu.roll(b3, s, 1) + b3, b3)
        a3 = jnp.where(keep, a3 * pltpu.roll(a3, s, 1), a3)
    scan_a[...] = a3.reshape(tile, GROUP)
    scan_b[...] = b3.reshape(tile, GROUP)

    def lru_step(g, carry):
        r0 = pl.multiple_of(g * SUBLANES, SUBLANES)
        h8 = scan_a[pl.ds(r0, SUBLANES), :] * carry + scan_b[pl.ds(r0, SUBLANES), :]
        scan_b[pl.ds(r0, SUBLANES), :] = h8
        return h8[SUBLANES - 1:SUBLANES, :]

    carry = lax.fori_loop(0, tile // SUBLANES, lru_step, lru_carry[0:1, :], unroll=True)
    lru_carry[...] = jnp.broadcast_to(carry, lru_carry.shape)
    yc = _rms(_gelu(proj(4)) * scan_b[...], lrunorm_ref[...])

    lb = lb_ref[...]
    qf = _silu(proj(5))
    sig = _logistic(proj(6))
    log_f = jnp.log(jnp.maximum(lb + (1.0 - lb) * sig, MIN_FORGET))
    kk = (1.0 - lb) * (1.0 - sig)
    vv = proj(7)
    csub = row % CHUNK
    bcum = log_f
    for s in (1, 2, 4, 8, 16, 32):
        bcum = jnp.where(csub >= s, bcum + pltpu.roll(bcum, s, 0), bcum)
    t_idx = lax.broadcasted_iota(jnp.int32, (CHUNK, GROUP), 0)
    s_idx = lax.broadcasted_iota(jnp.int32, (CHUNK, GROUP), 1) % HEAD_DIM
    causal = (t_idx >= s_idx).astype(F32)
    n_heads = GROUP // HEAD_DIM
    outs = []
    span = jnp.zeros((1, GROUP), F32)
    state = hgrn_state[...]
    state0_scr[...] = state
    for c in range(tile // CHUNK):
        sl = slice(c * CHUNK, (c + 1) * CHUNK)
        bc = bcum[sl]
        bmid = bc[CHUNK // 2 - 1:CHUNK // 2, :]
        blast = bc[CHUNK - 1:CHUNK, :]
        span = jnp.maximum(span, jnp.maximum(-bmid, bmid - blast))
        qm = qf[sl] * jnp.exp(bc - bmid)
        km = kk[sl] * jnp.exp(bmid - bc)
        qe = qm * jnp.exp(bmid)
        kl = km * jnp.exp(blast - bmid)
        vcb = vv[sl].astype(BF16)
        k_bd = (jnp.concatenate([km] * n_heads, axis=0) * bd_mask).astype(BF16)
        scores = _dot_nt(qm.astype(BF16), k_bd) * causal
        v_bd = (jnp.concatenate([vv[sl]] * n_heads, axis=0) * bd_mask).astype(BF16)
        outs.append(_dot(scores.astype(BF16), v_bd) + _dot_nt(qe.astype(BF16), state.astype(BF16)))
        state = state * jnp.exp(blast) + _dot_tn(vcb, kl.astype(BF16)) * bd_mask
    out_gate = _silu(proj(8))

    def hgrn_out(o):
        ms = _split_dot(o * o, bd_ones) * (1.0 / HEAD_DIM)
        return o * lax.rsqrt(ms + EPS) * hgrnnorm_ref[...] * out_gate

    yd = hgrn_out(jnp.concatenate(outs, axis=0))
    hgrn_state[...] = state
    ycat_scr[...] = jnp.concatenate(
        [ya.astype(BF16), yb.astype(BF16), yc.astype(BF16), yd.astype(BF16)], axis=1)
    o_ref[0] = x + gate_ref[0] * _dot(ycat_scr[...], wout_ref[...])

    @pl.when(jnp.max(span) > HGRN_SAFE_RANGE)
    def _():
        fb_scr[0] = qf
        fb_scr[1] = kk
        fb_scr[2] = vv
        fb_scr[3] = bcum
        t_row = lax.broadcasted_iota(jnp.int32, (CHUNK, 1), 0)

        def chunk_body(c, st):
            r0 = pl.multiple_of(c * CHUNK, CHUNK)
            q_c = fb_scr[0, pl.ds(r0, CHUNK), :]
            k_c = fb_scr[1, pl.ds(r0, CHUNK), :]
            v_c = fb_scr[2, pl.ds(r0, CHUNK), :]
            b_c = fb_scr[3, pl.ds(r0, CHUNK), :]
            b_end = fb_scr[3, pl.ds(r0 + CHUNK - 1, 1), :]

            def pair_body(s, acc):
                k_s = fb_scr[1, pl.ds(r0 + s, 1), :]
                v_s = fb_scr[2, pl.ds(r0 + s, 1), :]
                b_s = fb_scr[3, pl.ds(r0 + s, 1), :]
                w = jnp.where(t_row >= s, jnp.exp(jnp.minimum(b_c - b_s, 0.0)), 0.0)
                return acc + _split_dot(q_c * w * k_s, bd_ones) * v_s

            o_intra = lax.fori_loop(0, CHUNK, pair_body, jnp.zeros((CHUNK, GROUP), F32))
            o_inter = _dot_nt((q_c * jnp.exp(b_c)).astype(BF16), st.astype(BF16))
            o_scr[pl.ds(r0, CHUNK), :] = o_inter + o_intra
            k_end = k_c * jnp.exp(b_end - b_c)
            return st * jnp.exp(b_end) + _dot_tn(v_c.astype(BF16), k_end.astype(BF16)) * bd_mask

        hgrn_state[...] = lax.fori_loop(0, tile // CHUNK, chunk_body, state0_scr[...])
        ycat_scr[:, 3 * GROUP:] = hgrn_out(o_scr[...]).astype(BF16)
        o_ref[0] = x + gate_ref[0] * _dot(ycat_scr[...], wout_ref[...])


def _row(v):
    return v.reshape(1, -1).astype(F32)


def _block_diag(w):
    n, a, b = w.shape
    eye = jnp.eye(n, dtype=w.dtype)
    return (eye[:, None, :, None] * w[:, :, None, :]).reshape(n * a, n * b)


def _token_mix(x, sh, sc, gate, p):
    bsz, slen, d = x.shape
    tile = min(MIX_TILE, slen)
    full = lambda a: pl.BlockSpec(a.shape, lambda b, j: (0,) * a.ndim)
    per_batch = pl.BlockSpec((1, 1, d), lambda b, j: (b, 0, 0))
    x_spec = pl.BlockSpec((1, tile, d), lambda b, j: (b, j, 0))
    params = [p["norm"], p["w_in"], p["w_out"], p["pool_w"], p["pool_scale"], p["pool_norm"],
              p["sgu_w"], p["sgu_b"], p["sgu_norm"], p["conv_w"], p["conv_b"], p["lru_w"],
              p["lru_b"], p["lru_log"], p["lru_norm"], p["hgrn_lb"], p["hgrn_norm"]]
    return pl.pallas_call(
        _mix_kernel,
        grid=(bsz, slen // tile),
        in_specs=[x_spec, per_batch, per_batch, per_batch] + [full(a) for a in params],
        out_specs=x_spec,
        out_shape=jax.ShapeDtypeStruct(x.shape, x.dtype),
        scratch_shapes=[
            pltpu.VMEM((POOL_HIST, GROUP), F32),
            pltpu.VMEM((CONV_HIST, GROUP), F32),
            pltpu.VMEM((SUBLANES, GROUP), F32),
            pltpu.VMEM((GROUP, GROUP), F32),
            pltpu.VMEM((tile, GROUP), F32),
            pltpu.VMEM((tile, GROUP), F32),
            pltpu.VMEM((tile, N_IN_SLICES * GROUP), F32),
            pltpu.VMEM((GROUP, GROUP), F32),
            pltpu.VMEM((tile, GROUP), F32),
            pltpu.VMEM((tile, D_MODEL), BF16),
            pltpu.VMEM((4, tile, GROUP), F32),
        ],
        compiler_params=pltpu.CompilerParams(
            dimension_semantics=("arbitrary", "arbitrary"), vmem_limit_bytes=VMEM_LIMIT),
        name="token_mix",
    )(x, sh, sc, gate, *params)


def _ffn_kernel(x_ref, sh_ref, sc_ref, gate_ref, nw_ref, w1_ref, w3_ref, w2_ref, fw_ref,
                o_ref, h_scr, acc, *, final_norm):
    f = pl.program_id(1)

    @pl.when(f == 0)
    def _():
        h = _rms(x_ref[...], nw_ref[...]) * (1.0 + sc_ref[0]) + sh_ref[0]
        h_scr[...] = h.astype(BF16)
        acc[...] = jnp.zeros_like(acc)

    hb = h_scr[...]
    act = _silu(_dot(hb, w1_ref[...])) * _dot(hb, w3_ref[...])
    acc[...] += _dot(act.astype(BF16), w2_ref[...])

    @pl.when(f == pl.num_programs(1) - 1)
    def _():
        out = x_ref[...] + gate_ref[0] * acc[...]
        if final_norm:
            out = _rms(out, fw_ref[...])
        o_ref[...] = out


def _route_kernel(x_ref, sh_ref, sc_ref, nw_ref, rw_ref, rb_ref, h_ref, route_ref):
    h = _rms(x_ref[...], nw_ref[...]) * (1.0 + sc_ref[0]) + sh_ref[0]
    h_ref[...] = h
    h_hi = h.astype(BF16)
    h_lo = (h - h_hi.astype(F32)).astype(BF16)
    logits = (_dot(h_hi, rw_ref[0]) + _dot(h_lo, rw_ref[0]) + _dot(h_hi, rw_ref[1])) + rb_ref[...]
    lane = lax.broadcasted_iota(jnp.int32, logits.shape, 1)
    v1 = jnp.max(logits, axis=-1, keepdims=True)
    i1 = jnp.min(jnp.where(logits == v1, lane, ROUTER_LANES), axis=-1, keepdims=True)
    rest = jnp.where(lane == i1, NEG_BIG, logits)
    v2 = jnp.max(rest, axis=-1, keepdims=True)
    i2 = jnp.min(jnp.where(rest == v2, lane, ROUTER_LANES), axis=-1, keepdims=True)
    p2 = jnp.exp(v2 - v1)
    w_top = 1.0 / (1.0 + p2)
    route_ref[...] = jnp.where(lane == 0, i1.astype(F32), jnp.where(lane == 1, i2.astype(F32),
                               jnp.where(lane == 2, w_top, jnp.where(lane == 3, p2 * w_top, 0.0))))


def _gather_kernel(pad_ref, nu_ref, p1_ref, p2_ref, h_ref, xs_ref, zeros, sem, zsem):
    i = pl.program_id(0)
    chunk = h_ref.shape[0]
    tile = zeros.shape[0]
    n_tiles = xs_ref.shape[0] // tile

    @pl.when(i == 0)
    def _():
        zeros[...] = jnp.zeros_like(zeros)
        for e in range(pad_ref.shape[0]):
            cp = pltpu.make_async_copy(zeros, xs_ref.at[pl.ds(pad_ref[e], tile)], zsem)
            cp.start()
            cp.wait()

        def tail(t, carry):
            cp = pltpu.make_async_copy(
                zeros, xs_ref.at[pl.ds(pl.multiple_of(t * tile, tile), tile)], zsem)
            cp.start()
            cp.wait()
            return carry

        lax.fori_loop(nu_ref[0], n_tiles, tail, 0)

    def issue(r, carry):
        pltpu.make_async_copy(h_ref.at[pl.ds(r, 1)], xs_ref.at[p1_ref[0, 0, r]], sem).start(priority=0)
        pltpu.make_async_copy(h_ref.at[pl.ds(r, 1)], xs_ref.at[p2_ref[0, 0, r]], sem).start(priority=1)
        return carry

    lax.fori_loop(0, chunk, issue, 0, unroll=8)
    for _ in range(2):
        pltpu.make_async_copy(h_ref, xs_ref.at[pl.ds(0, chunk), 0], sem).wait()


def _moe_ffn_kernel(te_ref, nu_ref, xs_ref, w1_ref, w3_ref, w2_ref, ys_ref, xbuf, xsem, h_scr, acc):
    i = pl.program_id(0)
    f = pl.program_id(1)
    tm = xbuf.shape[1]

    def slab(t, slot):
        rows = pl.ds(pl.multiple_of(t * tm, tm), tm)
        return pltpu.make_async_copy(xs_ref.at[rows, 0], xbuf.at[slot], xsem.at[slot])

    @pl.when((i == 0) & (f == 0))
    def _():
        slab(0, 0).start()

    @pl.when(i < nu_ref[0])
    def _():
        @pl.when(f == 0)
        def _():
            slot = i % 2

            @pl.when(i + 1 < nu_ref[0])
            def _():
                slab(i + 1, 1 - slot).start()

            slab(i, slot).wait()
            h_scr[...] = xbuf[slot].astype(BF16)
            acc[...] = jnp.zeros_like(acc)

        hb = h_scr[...]
        act = _silu(_dot(hb, w1_ref[0])) * _dot(hb, w3_ref[0])
        acc[...] += _dot(act.astype(BF16), w2_ref[0])

        @pl.when(f == pl.num_programs(1) - 1)
        def _():
            ys_ref[...] = acc[...].reshape(ys_ref.shape)

    @pl.when((i >= nu_ref[0]) & (f == 0))
    def _():
        ys_ref[...] = jnp.zeros_like(ys_ref)


def _combine_kernel(p1_ref, p2_ref, x_ref, route_ref, gate_ref, fw_ref, ys_ref, o_ref,
                    ybuf, sem, *, final_norm):
    tm = x_ref.shape[0]

    def issue(r, carry):
        pltpu.make_async_copy(ys_ref.at[p1_ref[0, 0, r]], ybuf.at[0, pl.ds(r, 1)], sem).start(priority=0)
        pltpu.make_async_copy(ys_ref.at[p2_ref[0, 0, r]], ybuf.at[1, pl.ds(r, 1)], sem).start(priority=1)
        return carry

    lax.fori_loop(0, tm, issue, 0, unroll=8)
    for s in range(2):
        pltpu.make_async_copy(ys_ref.at[pl.ds(0, tm), 0], ybuf.at[s], sem).wait()
    route = route_ref[...]
    y = route[:, 2:3] * ybuf[0] + route[:, 3:4] * ybuf[1]
    out = x_ref[...] + gate_ref[0] * y
    if final_norm:
        out = _rms(out, fw_ref[...])
    o_ref[...] = out


def _ffn_mix(x, sh, sc, gate, norm_w, w1, w3, w2, final_w, *, final_norm):
    bsz, slen, d = x.shape
    tm = min(FFN_TILE_M, slen)
    per_seq = slen // tm
    d_ff = w1.shape[1]
    tf = min(FFN_TILE_F, d_ff)
    x2 = x.reshape(bsz * slen, d)
    x_spec = pl.BlockSpec((tm, d), lambda i, f: (i, 0))
    per_batch = pl.BlockSpec((1, 1, d), lambda i, f: (i // per_seq, 0, 0))
    row_spec = pl.BlockSpec((1, d), lambda i, f: (0, 0))
    out = pl.pallas_call(
        functools.partial(_ffn_kernel, final_norm=final_norm),
        grid=(bsz * per_seq, d_ff // tf),
        in_specs=[x_spec, per_batch, per_batch, per_batch, row_spec,
                  pl.BlockSpec((d, tf), lambda i, f: (0, f)),
                  pl.BlockSpec((d, tf), lambda i, f: (0, f)),
                  pl.BlockSpec((tf, d), lambda i, f: (f, 0)),
                  row_spec],
        out_specs=x_spec,
        out_shape=jax.ShapeDtypeStruct(x2.shape, x2.dtype),
        scratch_shapes=[pltpu.VMEM((tm, d), BF16), pltpu.VMEM((tm, d), F32)],
        compiler_params=pltpu.CompilerParams(
            dimension_semantics=("arbitrary", "arbitrary"), vmem_limit_bytes=VMEM_LIMIT),
        name="ffn_mix",
    )(x2, sh, sc, gate, norm_w, w1, w3, w2, final_w)
    return out.reshape(bsz, slen, d)


def _moe_mix(x, sh, sc, gate, norm_w, rw, rb, w1, w3, w2, final_w, *, final_norm):
    bsz, slen, d = x.shape
    n = bsz * slen
    n_e, _, d_ff = w1.shape
    x2 = x.reshape(n, d)

    tr = min(FFN_TILE_M, slen)
    per_seq = slen // tr
    per_batch = pl.BlockSpec((1, 1, d), lambda i: (i // per_seq, 0, 0))
    h3, route = pl.pallas_call(
        _route_kernel,
        grid=(n // tr,),
        in_specs=[pl.BlockSpec((tr, d), lambda i: (i, 0)), per_batch, per_batch,
                  pl.BlockSpec((1, d), lambda i: (0, 0)),
                  pl.BlockSpec(rw.shape, lambda i: (0, 0, 0)),
                  pl.BlockSpec(rb.shape, lambda i: (0, 0))],
        out_specs=[pl.BlockSpec((tr, d), lambda i: (i, 0)),
                   pl.BlockSpec((tr, ROUTER_LANES), lambda i: (i, 0))],
        out_shape=[jax.ShapeDtypeStruct((n, d), F32),
                   jax.ShapeDtypeStruct((n, ROUTER_LANES), F32)],
        compiler_params=pltpu.CompilerParams(vmem_limit_bytes=VMEM_LIMIT),
        name="moe_route",
    )(x2, sh, sc, norm_w, rw, rb)

    tm = min(MOE_TILE, n)
    n_tiles = 2 * n // tm + n_e
    rows = (n_tiles + 1) * tm
    e_flat = jnp.concatenate([route[:, 0], route[:, 1]]).astype(jnp.int32)
    onehot = (e_flat[:, None] == jnp.arange(n_e, dtype=jnp.int32)[None, :]).astype(jnp.int32)
    csum = jnp.cumsum(onehot, axis=0)
    counts = csum[-1]
    padded = ((counts + tm - 1) // tm) * tm
    gstart = jnp.cumsum(padded) - padded
    pos = jnp.sum(onehot * (gstart[None, :] + csum - 1), axis=1).astype(jnp.int32)
    pad_start = (gstart + counts).astype(jnp.int32)
    tile_end = jnp.cumsum(padded // tm)
    n_used = tile_end[-1:].astype(jnp.int32)
    tile_ids = jnp.minimum(jnp.arange(n_tiles, dtype=jnp.int32), n_used[0] - 1)
    tile_expert = jnp.sum((tile_ids[:, None] >= tile_end[None, :]).astype(jnp.int32), axis=1)

    chunk = min(GATHER_CHUNK, n)
    per_slot = n // chunk
    xs = pl.pallas_call(
        _gather_kernel,
        grid_spec=pltpu.PrefetchScalarGridSpec(
            num_scalar_prefetch=2,
            grid=(per_slot,),
            in_specs=[pl.BlockSpec((1, 1, chunk), lambda i, pad, nu: (i, 0, 0),
                                   memory_space=pltpu.SMEM),
                      pl.BlockSpec((1, 1, chunk), lambda i, pad, nu: (i + per_slot, 0, 0),
                                   memory_space=pltpu.SMEM),
                      pl.BlockSpec((chunk, d), lambda i, pad, nu: (i, 0))],
            out_specs=pl.BlockSpec(memory_space=pl.ANY),
            scratch_shapes=[pltpu.VMEM((tm, 1, d), F32), pltpu.SemaphoreType.DMA(()),
                            pltpu.SemaphoreType.DMA(())]),
        out_shape=jax.ShapeDtypeStruct((rows, 1, d), F32),
        compiler_params=pltpu.CompilerParams(
            dimension_semantics=("arbitrary",), vmem_limit_bytes=VMEM_LIMIT),
        name="moe_gather",
    )(pad_start, n_used, pos.reshape(2 * per_slot, 1, chunk), pos.reshape(2 * per_slot, 1, chunk), h3)

    tf = min(FFN_TILE_F, d_ff)
    n_f = d_ff // tf

    def f_idx(i, f, nu):
        return jnp.where(i < nu[0], f, n_f - 1)

    ys = pl.pallas_call(
        _moe_ffn_kernel,
        grid_spec=pltpu.PrefetchScalarGridSpec(
            num_scalar_prefetch=2,
            grid=(n_tiles, n_f),
            in_specs=[pl.BlockSpec(memory_space=pl.ANY),
                      pl.BlockSpec((1, d, tf), lambda i, f, te, nu: (te[i], 0, f_idx(i, f, nu))),
                      pl.BlockSpec((1, d, tf), lambda i, f, te, nu: (te[i], 0, f_idx(i, f, nu))),
                      pl.BlockSpec((1, tf, d), lambda i, f, te, nu: (te[i], f_idx(i, f, nu), 0))],
            out_specs=pl.BlockSpec((tm, 1, d), lambda i, f, te, nu: (i, 0, 0)),
            scratch_shapes=[pltpu.VMEM((2, tm, d), F32), pltpu.SemaphoreType.DMA((2,)),
                            pltpu.VMEM((tm, d), BF16), pltpu.VMEM((tm, d), F32)]),
        out_shape=jax.ShapeDtypeStruct((n_tiles * tm, 1, d), F32),
        compiler_params=pltpu.CompilerParams(
            dimension_semantics=("arbitrary", "arbitrary"), vmem_limit_bytes=VMEM_LIMIT),
        name="moe_ffn",
    )(tile_expert, n_used, xs, w1, w3, w2)

    tc = min(COMBINE_TILE, slen)
    per_seq_c = slen // tc
    n_c = n // tc
    pos3 = pos.reshape(2 * n_c, 1, tc)
    out = pl.pallas_call(
        functools.partial(_combine_kernel, final_norm=final_norm),
        grid=(n_c,),
        in_specs=[pl.BlockSpec((1, 1, tc), lambda i: (i, 0, 0), memory_space=pltpu.SMEM),
                  pl.BlockSpec((1, 1, tc), lambda i: (i + n_c, 0, 0), memory_space=pltpu.SMEM),
                  pl.BlockSpec((tc, d), lambda i: (i, 0)),
                  pl.BlockSpec((tc, ROUTER_LANES), lambda i: (i, 0)),
                  pl.BlockSpec((1, 1, d), lambda i: (i // per_seq_c, 0, 0)),
                  pl.BlockSpec((1, d), lambda i: (0, 0)),
                  pl.BlockSpec(memory_space=pl.ANY)],
        out_specs=pl.BlockSpec((tc, d), lambda i: (i, 0)),
        out_shape=jax.ShapeDtypeStruct((n, d), F32),
        scratch_shapes=[pltpu.VMEM((2, tc, d), F32), pltpu.SemaphoreType.DMA(())],
        compiler_params=pltpu.CompilerParams(
            dimension_semantics=("arbitrary",), vmem_limit_bytes=VMEM_LIMIT),
        name="moe_combine",
    )(pos3, pos3, x2, route, gate, final_w, ys)
    return out.reshape(bsz, slen, d)


def _mix_params(l, norm_mix, w_in, w_out, pool_w, pool_scale, pool_out_norm, sgu_ws, sgu_b,
                sgu_out_norm, conv_w, conv_b, lru_wa, lru_ba, lru_wx, lru_bx, lru_lambda,
                lru_out_norm, lower_bounds, hgrn_out_norm):
    tri = jnp.tril(jnp.ones((SGU_BLOCK, SGU_BLOCK), sgu_ws.dtype))
    sgu_w = jnp.concatenate([sgu_ws[l, hh] * tri for hh in range(sgu_ws.shape[1])], axis=1)
    return {
        "norm": _row(norm_mix[l]),
        "w_in": w_in[l].astype(BF16),
        "w_out": w_out[l].astype(BF16),
        "pool_w": _block_diag(pool_w[l]).astype(BF16),
        "pool_scale": _row(pool_scale[l]),
        "pool_norm": _row(pool_out_norm[l]),
        "sgu_w": sgu_w.astype(BF16),
        "sgu_b": jnp.repeat(sgu_b[l].T, HEAD_DIM, axis=1).astype(F32),
        "sgu_norm": _row(sgu_out_norm[l]),
        "conv_w": conv_w[l].astype(F32),
        "conv_b": _row(conv_b[l]),
        "lru_w": jnp.concatenate([_block_diag(lru_wa[l]), _block_diag(lru_wx[l])], axis=1).astype(BF16),
        "lru_b": _row(jnp.concatenate([lru_ba[l], lru_bx[l]])),
        "lru_log": _row(LRU_C * jax.nn.log_sigmoid(lru_lambda[l].astype(F32))),
        "lru_norm": _row(lru_out_norm[l]),
        "hgrn_lb": _row(lower_bounds[l]),
        "hgrn_norm": _row(hgrn_out_norm[l]),
    }


def kernel(x, c, w_mod, b_mod, norm_mix, norm_ffn, norm_final, w_in, w_out, pool_w, pool_scale, pool_out_norm, sgu_ws, sgu_b, sgu_out_norm, conv_w, conv_b, lru_wa, lru_ba, lru_wx, lru_bx, lru_lambda, lru_out_norm, hgrn_gamma, hgrn_out_norm, ffn_w1, ffn_w3, ffn_w2, router_w, router_b, moe_w1, moe_w3, moe_w2):
    depth = w_mod.shape[0]
    bsz, slen, d = x.shape
    p_layers = jax.nn.softmax(hgrn_gamma.astype(F32), axis=0)
    lower_bounds = jnp.cumsum(p_layers, axis=0) - p_layers[0:1]
    mod = _modulation(c, w_mod, b_mod)
    final_w = _row(norm_final)
    for l in range(depth):
        m = mod[l].reshape(bsz, 6, 1, d)
        sh1, sc1, g1, sh2, sc2, g2 = (m[:, i] for i in range(6))
        p = _mix_params(l, norm_mix, w_in, w_out, pool_w, pool_scale, pool_out_norm, sgu_ws,
                        sgu_b, sgu_out_norm, conv_w, conv_b, lru_wa, lru_ba, lru_wx, lru_bx,
                        lru_lambda, lru_out_norm, lower_bounds, hgrn_out_norm)
        x = _token_mix(x, sh1, sc1, g1, p)
        i = l // 2
        final_norm = l == depth - 1
        if l % 2 == 0:
            x = _ffn_mix(x, sh2, sc2, g2, _row(norm_ffn[l]), ffn_w1[i].astype(BF16),
                         ffn_w3[i].astype(BF16), ffn_w2[i].astype(BF16), final_w,
                         final_norm=final_norm)
        else:
            n_e = router_w.shape[2]
            rw = jnp.pad(router_w[i].astype(F32), ((0, 0), (0, ROUTER_LANES - n_e)))
            rw_hi = rw.astype(BF16)
            rw = jnp.stack([rw_hi, (rw - rw_hi.astype(F32)).astype(BF16)])
            rb = jnp.pad(router_b[i].astype(F32), (0, ROUTER_LANES - n_e),
                         constant_values=NEG_BIG).reshape(1, ROUTER_LANES)
            x = _moe_mix(x, sh2, sc2, g2, _row(norm_ffn[l]), rw, rb, moe_w1[i].astype(BF16),
                         moe_w3[i].astype(BF16), moe_w2[i].astype(BF16), final_w,
                         final_norm=final_norm)
    return x
```
